```python
import math
import jax
import jax.numpy as jnp
from jax import lax
import numpy as np

D_MODEL = 2048
BATCH = 4
SEQ = 2048
DEPTH = 2

NORM_EPS = 1e-6
MLSTM_HEADS = 8
MLSTM_DQK = D_MODEL // 16
MLSTM_DV = D_MODEL // 8
MLSTM_CHUNK = 64
SSD_D_INNER = D_MODEL
SSD_HEADDIM = 64
SSD_HEADS = SSD_D_INNER // SSD_HEADDIM
SSD_STATE = 128
SSD_GROUPS = 8
SSD_CONV = 4
SSD_CHUNK = 128
ATTN_HEADS = 16
ATTN_HEAD_DIM = D_MODEL // ATTN_HEADS
MOBA_BLOCK = 256
MOBA_TOPK = 3
MOBA_Q_CHUNK = 16
ROPE_THETA = 10000.0
D_FF = (8 * D_MODEL + 3 * 256 - 1) // (3 * 256) * 256

MLSTM_QK_W = MLSTM_HEADS * MLSTM_DQK
MLSTM_V_W = MLSTM_HEADS * MLSTM_DV
SSD_BC_W = SSD_GROUPS * SSD_STATE
SSD_XBC_W = SSD_D_INNER + 2 * SSD_BC_W
IN_SPLITS = (MLSTM_QK_W, MLSTM_QK_W, MLSTM_V_W, MLSTM_V_W, MLSTM_HEADS, MLSTM_HEADS,
             SSD_D_INNER, SSD_XBC_W, SSD_HEADS)
IN_COLS = sum(IN_SPLITS)
MIX_W = MLSTM_V_W + SSD_D_INNER

kernel_name = 'hybrid_mlstm_ssd_moba_block'


def rms_norm(x, g):
    xf = x.astype(jnp.float32)
    y = xf * lax.rsqrt(jnp.mean(xf * xf, axis=-1, keepdims=True) + NORM_EPS)
    return (y * g.astype(jnp.float32)).astype(x.dtype)


def swiglu(x, w_gate, w_up, w_down):
    return (jax.nn.silu(x @ w_gate) * (x @ w_up)) @ w_down


def rope(x, pos):
    half = x.shape[-1] // 2
    inv = ROPE_THETA ** (-jnp.arange(half, dtype=jnp.float32) / half)
    ang = pos.astype(jnp.float32)[:, None] * inv[None, :]
    cos, sin = jnp.cos(ang), jnp.sin(ang)
    x1, x2 = x[..., :half], x[..., half:]
    return jnp.concatenate([x1 * cos - x2 * sin, x2 * cos + x1 * sin], axis=-1)


def mlstm_chunkwise(q, k, v, i_pre, f_pre):
    b, h, s, dk = q.shape
    dv = v.shape[-1]
    L = MLSTM_CHUNK
    nc = s // L
    qc = (q * dk ** -0.5).reshape(b, h, nc, L, dk)
    kc = k.reshape(b, h, nc, L, dk)
    vc = v.reshape(b, h, nc, L, dv)
    ic = i_pre.reshape(b, h, nc, L)
    logf = jax.nn.log_sigmoid(f_pre).reshape(b, h, nc, L)
    bcum = jnp.cumsum(logf, axis=-1)
    causal = jnp.tril(jnp.ones((L, L), dtype=bool))
    log_d = jnp.where(causal, bcum[..., :, None] - bcum[..., None, :] + ic[..., None, :], -jnp.inf)
    log_w_end = bcum[..., -1:] - bcum + ic

    def step(carry, xs):
        C, n, m = carry
        k_c, v_c, lw, btot = xs
        m_new = jnp.maximum(btot + m, jnp.max(lw, axis=-1))
        decay = jnp.exp(btot + m - m_new)
        w = jnp.exp(lw - m_new[..., None])
        C_new = decay[..., None, None] * C + jnp.einsum('bhl,bhld,bhle->bhde', w, k_c, v_c)
        n_new = decay[..., None] * n + jnp.einsum('bhl,bhld->bhd', w, k_c)
        return (C_new, n_new, m_new), (C, n, m)

    init = (jnp.zeros((b, h, dk, dv), jnp.float32), jnp.zeros((b, h, dk), jnp.float32),
            jnp.zeros((b, h), jnp.float32))
    xs = (jnp.moveaxis(kc, 2, 0), jnp.moveaxis(vc, 2, 0), jnp.moveaxis(log_w_end, 2, 0),
          jnp.moveaxis(bcum[..., -1], 2, 0))
    _, (C_prev, n_prev, m_prev) = lax.scan(step, init, xs)
    C_prev = jnp.moveaxis(C_prev, 0, 2)
    n_prev = jnp.moveaxis(n_prev, 0, 2)
    m_prev = jnp.moveaxis(m_prev, 0, 2)

    log_inter = bcum + m_prev[..., None]
    m_row = jnp.maximum(log_inter, jnp.max(log_d, axis=-1))
    dmat = jnp.exp(log_d - m_row[..., None])
    scores = jnp.einsum('bhcld,bhcsd->bhcls', qc, kc) * dmat
    inter = jnp.exp(log_inter - m_row)
    num = (jnp.einsum('bhcls,bhcse->bhcle', scores, vc)
           + inter[..., None] * jnp.einsum('bhcld,bhcde->bhcle', qc, C_prev))
    nq = jnp.sum(scores, axis=-1) + inter * jnp.einsum('bhcld,bhcd->bhcl', qc, n_prev)
    denom = jnp.maximum(jnp.abs(nq), jnp.exp(-m_row))
    return (num / denom[..., None]).reshape(b, h, s, dv)


def segsum(a):
    L = a.shape[-1]
    cs = jnp.cumsum(a, axis=-1)
    mask = jnp.tril(jnp.ones((L, L), dtype=bool))
    return jnp.where(mask, cs[..., :, None] - cs[..., None, :], -jnp.inf)


def ssd_chunked(x, dt, a, bm, cm):
    b, s, h, p = x.shape
    g, n = bm.shape[2], bm.shape[3]
    r = h // g
    L = SSD_CHUNK
    nc = s // L
    xd = (x * dt[..., None]).reshape(b, nc, L, g, r, p)
    da = (dt * a).reshape(b, nc, L, g, r).transpose(0, 3, 4, 1, 2)
    bc = bm.reshape(b, nc, L, g, n)
    cc = cm.reshape(b, nc, L, g, n)
    da_cs = jnp.cumsum(da, axis=-1)
    lmat = jnp.exp(segsum(da))
    cb = jnp.einsum('bclgn,bcsgn->bcgls', cc, bc)
    y_diag = jnp.einsum('bcgls,bgrcls,bcsgrp->bclgrp', cb, lmat, xd)
    decay_end = jnp.exp(da_cs[..., -1:] - da_cs)
    chunk_states = jnp.einsum('bcsgn,bgrcs,bcsgrp->bcgrpn', bc, decay_end, xd)
    chunk_decay = jnp.exp(da_cs[..., -1])

    def step(state, xs):
        st_c, dec_c = xs
        return dec_c[..., None, None] * state + st_c, state

    init = jnp.zeros((b, g, r, p, n), jnp.float32)
    _, prev = lax.scan(step, init, (jnp.moveaxis(chunk_states, 1, 0), jnp.moveaxis(chunk_decay, 3, 0)))
    prev = jnp.moveaxis(prev, 0, 1)
    y_off = jnp.einsum('bclgn,bcgrpn,bgrcl->bclgrp', cc, prev, jnp.exp(da_cs))
    return (y_diag + y_off).reshape(b, s, h, p)


def causal_depthwise_conv(x, w, bias):
    kw = w.astype(x.dtype).reshape(SSD_CONV, 1, -1)
    y = lax.conv_general_dilated(x, kw, window_strides=(1,), padding=[(SSD_CONV - 1, 0)],
                                 dimension_numbers=('NWC', 'WIO', 'NWC'),
                                 feature_group_count=x.shape[-1])
    return y + bias.astype(x.dtype)


def mixer_mlstm_ssd(xn, w_in, mlstm_gate_bias, mlstm_norm, ssd_conv_w, ssd_conv_b,
                    ssd_dt_bias, ssd_a_log, ssd_d, ssd_norm, w_out):
    b, s, _ = xn.shape
    f32 = jnp.float32
    proj = (xn @ w_in).astype(f32)
    offs = np.cumsum(IN_SPLITS)[:-1].tolist()
    q, k, v, o, ig, fg, z, xbc, dt = jnp.split(proj, offs, axis=-1)

    def to_heads(t, d):
        return t.reshape(b, s, MLSTM_HEADS, d).transpose(0, 2, 1, 3)
    gb = mlstm_gate_bias.astype(f32)
    i_pre = (ig + gb[:MLSTM_HEADS]).transpose(0, 2, 1)
    f_pre = (fg + gb[MLSTM_HEADS:]).transpose(0, 2, 1)
    hm = mlstm_chunkwise(to_heads(q, MLSTM_DQK), to_heads(k, MLSTM_DQK), to_heads(v, MLSTM_DV), i_pre, f_pre)
    hm = hm.transpose(0, 2, 1, 3)
    hm = rms_norm(hm, mlstm_norm.reshape(MLSTM_HEADS, MLSTM_DV)) * jax.nn.sigmoid(o).reshape(b, s, MLSTM_HEADS, MLSTM_DV)
    hm = hm.reshape(b, s, MLSTM_V_W)

    xbc = jax.nn.silu(causal_depthwise_conv(xbc, ssd_conv_w, ssd_conv_b))
    xs, bm, cm = jnp.split(xbc, [SSD_D_INNER, SSD_D_INNER + SSD_BC_W], axis=-1)
    xs = xs.reshape(b, s, SSD_HEADS, SSD_HEADDIM)
    bm = bm.reshape(b, s, SSD_GROUPS, SSD_STATE)
    cm = cm.reshape(b, s, SSD_GROUPS, SSD_STATE)
    dt = jax.nn.softplus(dt + ssd_dt_bias.astype(f32))
    a = -jnp.exp(ssd_a_log.astype(f32))
    y = ssd_chunked(xs, dt, a, bm, cm) + ssd_d.astype(f32)[:, None] * xs
    y = y.reshape(b, s, SSD_D_INNER) * jax.nn.silu(z)
    y = rms_norm(y.reshape(b, s, SSD_GROUPS, SSD_D_INNER // SSD_GROUPS),
                 ssd_norm.reshape(SSD_GROUPS, SSD_D_INNER // SSD_GROUPS)).reshape(b, s, SSD_D_INNER)

    mixed = jnp.concatenate([hm, y], axis=-1).astype(xn.dtype)
    return mixed @ w_out


def moba_attention(q, k, v):
    b, h, s, hd = q.shape
    nb = -(-s // MOBA_BLOCK)
    pad = nb * MOBA_BLOCK - s
    k_pad = jnp.pad(k, ((0, 0), (0, 0), (0, pad), (0, 0)))
    v_pad = jnp.pad(v, ((0, 0), (0, 0), (0, pad), (0, 0)))
    k_blocks = k_pad.reshape(b, h, nb, MOBA_BLOCK, hd)
    v_blocks = v_pad.reshape(b, h, nb, MOBA_BLOCK, hd)
    scale = hd ** -0.5
    k_mean = jnp.mean(k_blocks, axis=3)
    q_blk = jnp.arange(s) // MOBA_BLOCK
    gate = jnp.einsum('bhsd,bhnd->bhsn', q, k_mean)
    past = jnp.arange(nb)[None, :] < q_blk[:, None]
    gate = jnp.where(past, gate, -jnp.inf)
    topk = min(MOBA_TOPK, nb)
    _, sel = lax.top_k(gate, topk)
    sel_valid = jnp.arange(topk)[None, :] < q_blk[:, None]
    b_idx = jnp.arange(b)[:, None, None, None]
    h_idx = jnp.arange(h)[None, :, None, None]
    QC = MOBA_Q_CHUNK

    def chunk(c):
        start = c * QC
        q_c = lax.dynamic_slice_in_dim(q, start, QC, axis=2)
        sel_c = lax.dynamic_slice_in_dim(sel, start, QC, axis=2)
        valid_c = lax.dynamic_slice_in_dim(sel_valid, start, QC, axis=0)
        own = start // MOBA_BLOCK
        k_own = lax.dynamic_slice_in_dim(k_pad, own * MOBA_BLOCK, MOBA_BLOCK, axis=2)
        v_own = lax.dynamic_slice_in_dim(v_pad, own * MOBA_BLOCK, MOBA_BLOCK, axis=2)
        k_sel = k_blocks[b_idx, h_idx, sel_c]
        v_sel = v_blocks[b_idx, h_idx, sel_c]
        s_sel = jnp.einsum('bhqd,bhqjkd->bhqjk', q_c, k_sel) * scale
        s_sel = jnp.where(valid_c[None, None, :, :, None], s_sel, -jnp.inf)
        s_own = jnp.einsum('bhqd,bhkd->bhqk', q_c, k_own) * scale
        q_pos = start + jnp.arange(QC)
        k_pos = own * MOBA_BLOCK + jnp.arange(MOBA_BLOCK)
        s_own = jnp.where(k_pos[None, :] <= q_pos[:, None], s_own, -jnp.inf)
        logits = jnp.concatenate([s_sel.reshape(b, h, QC, topk * MOBA_BLOCK), s_own], axis=-1)
        p = jax.nn.softmax(logits, axis=-1)
        p_sel = p[..., :topk * MOBA_BLOCK].reshape(b, h, QC, topk, MOBA_BLOCK)
        p_own = p[..., topk * MOBA_BLOCK:]
        return (jnp.einsum('bhqjk,bhqjkd->bhqd', p_sel, v_sel)
                + jnp.einsum('bhqk,bhkd->bhqd', p_own, v_own))

    out = lax.map(chunk, jnp.arange(s // QC))
    return out.transpose(1, 2, 0, 3, 4).reshape(b, h, s, hd)


def mixer_moba(xn, w_qkv, w_o):
    b, s, _ = xn.shape
    qkv = (xn @ w_qkv).astype(jnp.float32).reshape(b, s, 3, ATTN_HEADS, ATTN_HEAD_DIM)
    q = qkv[:, :, 0].transpose(0, 2, 1, 3)
    k = qkv[:, :, 1].transpose(0, 2, 1, 3)
    v = qkv[:, :, 2].transpose(0, 2, 1, 3)
    pos = jnp.arange(s)
    o = moba_attention(rope(q, pos), rope(k, pos), v)
    o = o.transpose(0, 2, 1, 3).reshape(b, s, D_MODEL).astype(xn.dtype)
    return o @ w_o


def setup_inputs(seed: int = 0) -> dict:
    key = jax.random.key(seed)
    ks = jax.random.split(key, 32)
    f32 = jnp.float32

    def dense(k, fan_in, fan_out):
        return jax.random.normal(k, (fan_in, fan_out), f32) * fan_in ** -0.5

    def gain(k, n):
        return 1.0 + 0.02 * jax.random.normal(k, (n,), f32)

    x = jax.random.normal(ks[0], (BATCH, SEQ, D_MODEL), f32)
    i_bias = 0.1 * jax.random.normal(ks[1], (MLSTM_HEADS,), f32)
    f_bias = jnp.linspace(3.0, 6.0, MLSTM_HEADS, dtype=f32) + 0.1 * jax.random.normal(ks[2], (MLSTM_HEADS,), f32)
    u = jax.random.uniform(ks[3], (SSD_HEADS,), f32)
    dt0 = jnp.exp(u * (math.log(0.1) - math.log(1e-3)) + math.log(1e-3))
    dt_bias = dt0 + jnp.log(-jnp.expm1(-dt0))
    a_log = jnp.log(jax.random.uniform(ks[4], (SSD_HEADS,), f32, 1.0, 16.0))
    return {
        'x': x,
        'l0_norm_mix': gain(ks[5], D_MODEL),
        'l0_w_in': dense(ks[6], D_MODEL, IN_COLS),
        'l0_mlstm_gate_bias': jnp.concatenate([i_bias, f_bias]),
        'l0_mlstm_norm': gain(ks[7], MLSTM_V_W),
        'l0_ssd_conv_w': jax.random.normal(ks[8], (SSD_CONV, SSD_XBC_W), f32) * SSD_CONV ** -0.5,
        'l0_ssd_conv_b': 0.02 * jax.random.normal(ks[9], (SSD_XBC_W,), f32),
        'l0_ssd_dt_bias': dt_bias,
        'l0_ssd_a_log': a_log,
        'l0_ssd_d': 1.0 + 0.1 * jax.random.normal(ks[10], (SSD_HEADS,), f32),
        'l0_ssd_norm': gain(ks[11], SSD_D_INNER),
        'l0_w_out': dense(ks[12], MIX_W, D_MODEL),
        'l0_norm_ffn': gain(ks[13], D_MODEL),
        'l0_ffn_gate': dense(ks[14], D_MODEL, D_FF),
        'l0_ffn_up': dense(ks[15], D_MODEL, D_FF),
        'l0_ffn_down': dense(ks[16], D_FF, D_MODEL),
        'l1_norm_mix': gain(ks[17], D_MODEL),
        'l1_w_qkv': dense(ks[18], D_MODEL, 3 * D_MODEL),
        'l1_w_o': dense(ks[19], D_MODEL, D_MODEL),
        'l1_norm_ffn': gain(ks[20], D_MODEL),
        'l1_ffn_gate': dense(ks[21], D_MODEL, D_FF),
        'l1_ffn_up': dense(ks[22], D_MODEL, D_FF),
        'l1_ffn_down': dense(ks[23], D_FF, D_MODEL),
        'final_norm': gain(ks[24], D_MODEL),
    }


def reference(x, l0_norm_mix, l0_w_in, l0_mlstm_gate_bias, l0_mlstm_norm, l0_ssd_conv_w,
              l0_ssd_conv_b, l0_ssd_dt_bias, l0_ssd_a_log, l0_ssd_d, l0_ssd_norm, l0_w_out,
              l0_norm_ffn, l0_ffn_gate, l0_ffn_up, l0_ffn_down,
              l1_norm_mix, l1_w_qkv, l1_w_o, l1_norm_ffn, l1_ffn_gate, l1_ffn_up, l1_ffn_down,
              final_norm):
    norm_mix = (l0_norm_mix, l1_norm_mix)
    norm_ffn = (l0_norm_ffn, l1_norm_ffn)
    mixer_params = ((l0_w_in, l0_mlstm_gate_bias, l0_mlstm_norm, l0_ssd_conv_w, l0_ssd_conv_b,
                     l0_ssd_dt_bias, l0_ssd_a_log, l0_ssd_d, l0_ssd_norm, l0_w_out),
                    (l1_w_qkv, l1_w_o))
    ffn_params = ((l0_ffn_gate, l0_ffn_up, l0_ffn_down), (l1_ffn_gate, l1_ffn_up, l1_ffn_down))
    for layer in range(DEPTH):
        xn = rms_norm(x, norm_mix[layer])
        if layer % 2 == 0:
            x = x + mixer_mlstm_ssd(xn, *mixer_params[layer])
        else:
            x = x + mixer_moba(xn, *mixer_params[layer])
        x = x + swiglu(rms_norm(x, norm_ffn[layer]), *ffn_params[layer])
    return rms_norm(x, final_norm)
```

```python
import functools
import math

import jax
import jax.numpy as jnp
import numpy as np
from jax import lax
from jax.experimental import pallas as pl
from jax.experimental.pallas import tpu as pltpu

F32 = jnp.float32
BF16 = jnp.bfloat16

D_MODEL = 2048
NORM_EPS = 1e-6
MLSTM_HEADS = 8
MLSTM_DQK = 128
MLSTM_DV = 256
SSD_D_INNER = 2048
SSD_HEADDIM = 64
SSD_HEADS = 32
SSD_STATE = 128
SSD_GROUPS = 8
SSD_HEADS_PER_GROUP = SSD_HEADS // SSD_GROUPS
SSD_GROUP_W = SSD_HEADS_PER_GROUP * SSD_HEADDIM
SSD_CONV = 4
ATTN_HEADS = 16
ATTN_HEAD_DIM = 128
MOBA_BLOCK = 256
MOBA_TOPK = 3
ROPE_THETA = 10000.0
D_FF = 5632

OFF_Q, OFF_K, OFF_V, OFF_O, OFF_Z, OFF_X, OFF_B, OFF_C = 0, 1024, 2048, 4096, 6144, 8192, 10240, 11264
PROJ_W = 12288
GATE_I, GATE_F, GATE_DT, GATE_CS = 0, 8, 16, 48
GATE_W = 128

CHUNK = 256
CONV_HALO = 8

V7X_VMEM_LIMIT_BYTES = 56 * 1024 * 1024


def _cparams(sem):
    return pltpu.CompilerParams(dimension_semantics=sem, vmem_limit_bytes=V7X_VMEM_LIMIT_BYTES)


def _sigmoid(x):
    return 1.0 / (1.0 + jnp.exp(-x))


def _silu(x):
    return x * _sigmoid(x)


def _softplus(x):
    return jnp.maximum(x, 0.0) + jnp.log(1.0 + jnp.exp(-jnp.abs(x)))


def _rms(x, g):
    ms = jnp.mean(x * x, axis=-1, keepdims=True)
    return x * lax.rsqrt(ms + NORM_EPS) * g


def _dot(a, b):
    return jnp.dot(a, b, preferred_element_type=F32)


def _dot_nt(a, b):
    return lax.dot_general(a, b, (((1,), (1,)), ((), ())), preferred_element_type=F32)


def _norm_matmul_kernel(x_ref, g_ref, w_ref, o_ref, xn_ref):
    @pl.when(pl.program_id(1) == 0)
    def _():
        xn_ref[...] = _rms(x_ref[...], g_ref[...]).astype(BF16)

    o_ref[...] = _dot(xn_ref[...], w_ref[...]).astype(o_ref.dtype)


def norm_matmul(x, g, w, *, tm, tn, out_dtype=F32, name):
    m, k = x.shape
    n = w.shape[1]
    return pl.pallas_call(
        _norm_matmul_kernel,
        grid=(m // tm, n // tn),
        in_specs=[pl.BlockSpec((tm, k), lambda i, j: (i, 0)),
                  pl.BlockSpec((1, k), lambda i, j: (0, 0)),
                  pl.BlockSpec((k, tn), lambda i, j: (0, j))],
        out_specs=pl.BlockSpec((tm, tn), lambda i, j: (i, j)),
        out_shape=jax.ShapeDtypeStruct((m, n), out_dtype),
        scratch_shapes=[pltpu.VMEM((tm, k), BF16)],
        compiler_params=_cparams(("parallel", "arbitrary")),
        name=name,
    )(x, g.reshape(1, k), w)


def _swiglu_up_kernel(x_ref, g_ref, wg_ref, wu_ref, o_ref, xn_ref):
    @pl.when(pl.program_id(1) == 0)
    def _():
        xn_ref[...] = _rms(x_ref[...], g_ref[...]).astype(BF16)

    xn = xn_ref[...]
    gate = _dot(xn, wg_ref[...])
    up = _dot(xn, wu_ref[...])
    o_ref[...] = (_silu(gate) * up).astype(o_ref.dtype)


def swiglu_up(x, g, wg, wu, *, tm, tn, name):
    m, k = x.shape
    n = wg.shape[1]
    return pl.pallas_call(
        _swiglu_up_kernel,
        grid=(m // tm, n // tn),
        in_specs=[pl.BlockSpec((tm, k), lambda i, j: (i, 0)),
                  pl.BlockSpec((1, k), lambda i, j: (0, 0)),
                  pl.BlockSpec((k, tn), lambda i, j: (0, j)),
                  pl.BlockSpec((k, tn), lambda i, j: (0, j))],
        out_specs=pl.BlockSpec((tm, tn), lambda i, j: (i, j)),
        out_shape=jax.ShapeDtypeStruct((m, n), BF16),
        scratch_shapes=[pltpu.VMEM((tm, k), BF16)],
        compiler_params=_cparams(("parallel", "arbitrary")),
        name=name,
    )(x, g.reshape(1, k), wg, wu)


def _matmul_residual_kernel(a_ref, w_ref, r_ref, o_ref):
    o_ref[...] = r_ref[...] + _dot(a_ref[...], w_ref[...])


def matmul_residual(a, w, res, *, tm, tn, name):
    m, k = a.shape
    n = w.shape[1]
    return pl.pallas_call(
        _matmul_residual_kernel,
        grid=(m // tm, n // tn),
        in_specs=[pl.BlockSpec((tm, k), lambda i, j: (i, 0)),
                  pl.BlockSpec((k, tn), lambda i, j: (0, j)),
                  pl.BlockSpec((tm, tn), lambda i, j: (i, j))],
        out_specs=pl.BlockSpec((tm, tn), lambda i, j: (i, j)),
        out_shape=jax.ShapeDtypeStruct((m, n), F32),
        compiler_params=_cparams(("parallel", "parallel")),
        name=name,
    )(a, w, res)


def _rmsnorm_kernel(x_ref, g_ref, o_ref):
    o_ref[...] = _rms(x_ref[...], g_ref[...])


def rmsnorm(x, g, *, tm, name):
    m, k = x.shape
    return pl.pallas_call(
        _rmsnorm_kernel,
        grid=(m // tm,),
        in_specs=[pl.BlockSpec((tm, k), lambda i: (i, 0)),
                  pl.BlockSpec((1, k), lambda i: (0, 0))],
        out_specs=pl.BlockSpec((tm, k), lambda i: (i, 0)),
        out_shape=jax.ShapeDtypeStruct((m, k), F32),
        compiler_params=_cparams(("parallel",)),
        name=name,
    )(x, g.reshape(1, k))


def _cumsum_rows(x):
    n = x.shape[0]
    row = lax.broadcasted_iota(jnp.int32, x.shape, 0)
    shift = 1
    while shift < n:
        x = x + jnp.where(row >= shift, pltpu.roll(x, shift, axis=0), 0.0)
        shift *= 2
    return x


def _gate_prep_kernel(raw_ref, bias_ref, alog_ref, col_ref, row_ref):
    v = raw_ref[0] + bias_ref[...]
    lane = lax.broadcasted_iota(jnp.int32, v.shape, 1)
    is_f = (lane >= GATE_F) & (lane < GATE_DT)
    is_dt = (lane >= GATE_DT) & (lane < GATE_CS)
    logf = -_softplus(-v)
    dt = _softplus(v)
    da = jnp.where(is_dt, dt * (-jnp.exp(alog_ref[...])), 0.0)
    pre = jnp.where(is_f, logf, 0.0) + pltpu.roll(da, GATE_CS - GATE_DT, axis=1)
    cum = _cumsum_rows(pre)
    out = jnp.where(lane < GATE_F, v, jnp.where(is_f, cum, jnp.where(is_dt, dt, cum)))
    col_ref[0] = out
    row_ref[0] = out.T


def gate_prep(raw, bias_row, alog_row):
    b, s, w = raw.shape
    return pl.pallas_call(
        _gate_prep_kernel,
        grid=(b, s // CHUNK),
        in_specs=[pl.BlockSpec((1, CHUNK, w), lambda i, c: (i, c, 0)),
                  pl.BlockSpec((1, w), lambda i, c: (0, 0)),
                  pl.BlockSpec((1, w), lambda i, c: (0, 0))],
        out_specs=[pl.BlockSpec((1, CHUNK, w), lambda i, c: (i, c, 0)),
                   pl.BlockSpec((1, w, CHUNK), lambda i, c: (i, 0, c))],
        out_shape=[jax.ShapeDtypeStruct((b, s, w), F32),
                   jax.ShapeDtypeStruct((b, w, s), F32)],
        compiler_params=_cparams(("parallel", "parallel")),
        name="gate_prep",
    )(raw, bias_row, alog_row)


def _lane_column(tile, idx):
    lane = lax.broadcasted_iota(jnp.int32, tile.shape, 1)
    return jnp.sum(jnp.where(lane == idx, tile, 0.0), axis=1, keepdims=True)


def _mlstm_kernel(q_ref, k_ref, v_ref, o_ref, gcol_ref, grow_ref, gain_ref, out_ref):
    h = pl.program_id(1)
    s_len = q_ref.shape[1]
    L = CHUNK
    scale = MLSTM_DQK ** -0.5
    ri = lax.broadcasted_iota(jnp.int32, (L, L), 0)
    ci = lax.broadcasted_iota(jnp.int32, (L, L), 1)
    causal = ci <= ri
    gain = gain_ref[0]

    c_state = jnp.zeros((MLSTM_DQK, MLSTM_DV), F32)
    n_state = jnp.zeros((1, MLSTM_DQK), F32)
    m_state = jnp.zeros((1, 1), F32)

    for c in range(s_len // L):
        rows = pl.ds(c * L, L)
        q = q_ref[0, rows, :] * scale
        k = k_ref[0, rows, :]
        qb = q.astype(BF16)
        kb = k.astype(BF16)
        vb = v_ref[0, rows, :].astype(BF16)
        gc = gcol_ref[0, rows, :]
        i_col = _lane_column(gc, GATE_I + h)
        b_col = _lane_column(gc, GATE_F + h)
        i_row = grow_ref[0, pl.ds(GATE_I + h, 1), rows]
        b_row = grow_ref[0, pl.ds(GATE_F + h, 1), rows]

        log_d = jnp.where(causal, b_col - (b_row - i_row), -jnp.inf)
        log_inter = b_col + m_state
        m_row = jnp.maximum(log_inter, jnp.max(log_d, axis=1, keepdims=True))
        dmat = jnp.exp(log_d - m_row)
        scores = _dot_nt(qb, kb) * dmat
        inter = jnp.exp(log_inter - m_row)
        num = _dot(scores.astype(BF16), vb) + inter * _dot(qb, c_state.astype(BF16))
        nq = jnp.sum(scores, axis=1, keepdims=True) + inter * jnp.sum(q * n_state, axis=1, keepdims=True)
        denom = jnp.maximum(jnp.abs(nq), jnp.exp(-m_row))
        hm = num / denom
        y = _rms(hm, gain) * _sigmoid(o_ref[0, rows, :])
        out_ref[0, rows, :] = y.astype(out_ref.dtype)

        btot = b_col[L - 1:L, :]
        lw = btot - b_col + i_col
        m_new = jnp.maximum(btot + m_state, jnp.max(lw, axis=0, keepdims=True))
        decay = jnp.exp(btot + m_state - m_new)
        wk = jnp.exp(lw - m_new) * k
        c_state = decay * c_state + _dot(wk.T.astype(BF16), vb)
        n_state = decay * n_state + jnp.sum(wk, axis=0, keepdims=True)
        m_state = m_new


def mlstm(proj, gcol, grow, gain):
    b, s, _ = proj.shape
    qk_blk = lambda off: pl.BlockSpec((1, s, MLSTM_DQK), lambda i, h: (i, 0, off // MLSTM_DQK + h))
    v_blk = lambda off: pl.BlockSpec((1, s, MLSTM_DV), lambda i, h: (i, 0, off // MLSTM_DV + h))
    return pl.pallas_call(
        _mlstm_kernel,
        grid=(b, MLSTM_HEADS),
        in_specs=[qk_blk(OFF_Q), qk_blk(OFF_K), v_blk(OFF_V), v_blk(OFF_O),
                  pl.BlockSpec((1, s, GATE_W), lambda i, h: (i, 0, 0)),
                  pl.BlockSpec((1, GATE_W, s), lambda i, h: (i, 0, 0)),
                  pl.BlockSpec((1, 1, MLSTM_DV), lambda i, h: (h, 0, 0))],
        out_specs=pl.BlockSpec((1, s, MLSTM_DV), lambda i, h: (i, 0, h)),
        out_shape=jax.ShapeDtypeStruct((b, s, MLSTM_HEADS * MLSTM_DV), BF16),
        compiler_params=_cparams(("parallel", "parallel")),
        name="mlstm",
    )(proj, proj, proj, proj, gcol, grow, gain.reshape(MLSTM_HEADS, 1, MLSTM_DV))


def _conv_silu(ref, c, w_ref, b_ref):
    L = CHUNK
    cur = ref[0, pl.ds(c * L, L), :]
    if c == 0:
        halo = jnp.zeros((CONV_HALO, cur.shape[1]), F32)
    else:
        halo = ref[0, pl.ds(c * L - CONV_HALO, CONV_HALO), :]
    ext = jnp.concatenate([halo, cur], axis=0)
    w = w_ref[...]
    acc = b_ref[...] + w[SSD_CONV - 1:SSD_CONV, :] * cur
    for back in range(1, SSD_CONV):
        shifted = pltpu.roll(ext, back, axis=0)[CONV_HALO:, :]
        acc = acc + w[SSD_CONV - 1 - back:SSD_CONV - back, :] * shifted
    return _silu(acc)


def _per_head_lanes(vals, shape):
    lane = lax.broadcasted_iota(jnp.int32, shape, 1)
    out = jnp.broadcast_to(vals[SSD_HEADS_PER_GROUP - 1], shape)
    for r in range(SSD_HEADS_PER_GROUP - 2, -1, -1):
        out = jnp.where(lane < (r + 1) * SSD_HEADDIM, vals[r], out)
    return out


def _ssd_kernel(x_ref, b_ref, c_ref, z_ref, wx_ref, wb_ref, wc_ref, bx_ref, bb_ref, bc_ref,
                gcol_ref, grow_ref, d_ref, gain_ref, out_ref):
    g = pl.program_id(1)
    s_len = x_ref.shape[1]
    L = CHUNK
    R = SSD_HEADS_PER_GROUP
    ri = lax.broadcasted_iota(jnp.int32, (L, L), 0)
    ci = lax.broadcasted_iota(jnp.int32, (L, L), 1)
    causal = ci <= ri
    lane_head = lax.broadcasted_iota(jnp.int32, (L, SSD_GROUP_W), 1) // SSD_HEADDIM

    state = jnp.zeros((SSD_STATE, SSD_GROUP_W), F32)

    for c in range(s_len // L):
        rows = pl.ds(c * L, L)
        xs = _conv_silu(x_ref, c, wx_ref, bx_ref)
        bm = _conv_silu(b_ref, c, wb_ref, bb_ref)
        cm = _conv_silu(c_ref, c, wc_ref, bc_ref)
        bmb = bm.astype(BF16)
        cmb = cm.astype(BF16)
        gc = gcol_ref[0, rows, :]
        dt_cols = [_lane_column(gc, GATE_DT + R * g + r) for r in range(R)]
        cs_cols = [_lane_column(gc, GATE_CS + R * g + r) for r in range(R)]
        cs_rows = [grow_ref[0, pl.ds(GATE_CS + R * g + r, 1), rows] for r in range(R)]

        dt_full = _per_head_lanes(dt_cols, (L, SSD_GROUP_W))
        cs_full = _per_head_lanes(cs_cols, (L, SSD_GROUP_W))
        cs_end = _per_head_lanes([col[L - 1:L, :] for col in cs_cols], (1, SSD_GROUP_W))

        xd = xs * dt_full
        xdb = xd.astype(BF16)
        cb = _dot_nt(cmb, bmb)
        y = _dot(cmb, state.astype(BF16)) * jnp.exp(cs_full)
        for r in range(R):
            lmat = jnp.where(causal, jnp.exp(jnp.minimum(cs_cols[r] - cs_rows[r], 0.0)), 0.0)
            xr = jnp.where(lane_head == r, xdb, jnp.zeros_like(xdb))
            y = y + _dot((cb * lmat).astype(BF16), xr)
        y = y + d_ref[...] * xs
        y = y * _silu(z_ref[0, rows, :])
        out_ref[0, rows, :] = _rms(y, gain_ref[...]).astype(out_ref.dtype)

        decay_end = jnp.exp(cs_end - cs_full)
        state = jnp.exp(cs_end) * state + _dot(bm.T.astype(BF16), (xd * decay_end).astype(BF16))


def ssd(proj, conv_w, conv_b, gcol, grow, d_row, gain_row):
    b, s, _ = proj.shape
    G = SSD_GROUPS
    wide = lambda off: pl.BlockSpec((1, s, SSD_GROUP_W), lambda i, g: (i, 0, off // SSD_GROUP_W + g))
    narrow = lambda off: pl.BlockSpec((1, s, SSD_STATE), lambda i, g: (i, 0, off // SSD_STATE + g))
    cw = lambda rows, width, off: pl.BlockSpec((rows, width), lambda i, g: (0, off // width + g))
    return pl.pallas_call(
        _ssd_kernel,
        grid=(b, G),
        in_specs=[wide(OFF_X), narrow(OFF_B), narrow(OFF_C), wide(OFF_Z),
                  cw(SSD_CONV, SSD_GROUP_W, 0), cw(SSD_CONV, SSD_STATE, 2048), cw(SSD_CONV, SSD_STATE, 3072),
                  cw(1, SSD_GROUP_W, 0), cw(1, SSD_STATE, 2048), cw(1, SSD_STATE, 3072),
                  pl.BlockSpec((1, s, GATE_W), lambda i, g: (i, 0, 0)),
                  pl.BlockSpec((1, GATE_W, s), lambda i, g: (i, 0, 0)),
                  cw(1, SSD_GROUP_W, 0), cw(1, SSD_GROUP_W, 0)],
        out_specs=pl.BlockSpec((1, s, SSD_GROUP_W), lambda i, g: (i, 0, g)),
        out_shape=jax.ShapeDtypeStruct((b, s, SSD_D_INNER), BF16),
        compiler_params=_cparams(("parallel", "parallel")),
        name="ssd",
    )(proj, proj, proj, proj, conv_w, conv_w, conv_w, conv_b, conv_b, conv_b, gcol, grow, d_row, gain_row)


def _moba_kernel(q_ref, k_ref, v_ref, cos_ref, sin_ref, out_ref, qs_ref, ks_ref, vs_ref):
    s_len = q_ref.shape[1]
    BLK = MOBA_BLOCK
    nb = s_len // BLK
    half = ATTN_HEAD_DIM // 2
    scale = ATTN_HEAD_DIM ** -0.5
    cos = cos_ref[...]
    sin = sin_ref[...]

    def rope(x):
        return x * cos + pltpu.roll(x, half, axis=1) * sin

    qs_ref[...] = rope(q_ref[0])
    ks_ref[...] = rope(k_ref[0])
    vs_ref[...] = v_ref[0].astype(BF16)

    ri = lax.broadcasted_iota(jnp.int32, (BLK, BLK), 0)
    ci = lax.broadcasted_iota(jnp.int32, (BLK, BLK), 1)
    causal = ci <= ri

    k_means = [jnp.mean(ks_ref[pl.ds(j * BLK, BLK), :], axis=0, keepdims=True) for j in range(nb)]

    for i in range(nb):
        q = qs_ref[pl.ds(i * BLK, BLK), :]
        qb = q.astype(BF16)
        if i <= MOBA_TOPK:
            keep = [None] * i
        else:
            gate = [jnp.sum(q * k_means[j], axis=1, keepdims=True) for j in range(i)]
            keep = []
            for j in range(i):
                rank = jnp.zeros((BLK, 1), F32)
                for m in range(i):
                    if m == j:
                        continue
                    ahead = (gate[m] >= gate[j]) if m < j else (gate[m] > gate[j])
                    rank = rank + jnp.where(ahead, 1.0, 0.0)
                keep.append(rank < float(MOBA_TOPK))

        logits = []
        for j in range(i + 1):
            s = _dot_nt(qb, ks_ref[pl.ds(j * BLK, BLK), :].astype(BF16)) * scale
            if j == i:
                s = jnp.where(causal, s, -jnp.inf)
            elif keep[j] is not None:
                s = jnp.where(keep[j], s, -jnp.inf)
            logits.append(s)
        m_row = functools.reduce(jnp.maximum, [jnp.max(s, axis=1, keepdims=True) for s in logits])
        acc = jnp.zeros((BLK, ATTN_HEAD_DIM), F32)
        den = jnp.zeros((BLK, 1), F32)
        for j in range(i + 1):
            p = jnp.exp(logits[j] - m_row)
            den = den + jnp.sum(p, axis=1, keepdims=True)
            acc = acc + _dot(p.astype(BF16), vs_ref[pl.ds(j * BLK, BLK), :])
        out_ref[0, pl.ds(i * BLK, BLK), :] = (acc / den).astype(out_ref.dtype)


def moba(qkv, cos_full, sin_signed):
    b, s, _ = qkv.shape
    hd = ATTN_HEAD_DIM
    blk = lambda part: pl.BlockSpec((1, s, hd), lambda i, h: (i, 0, part * ATTN_HEADS + h))
    return pl.pallas_call(
        _moba_kernel,
        grid=(b, ATTN_HEADS),
        in_specs=[blk(0), blk(1), blk(2),
                  pl.BlockSpec((s, hd), lambda i, h: (0, 0)),
                  pl.BlockSpec((s, hd), lambda i, h: (0, 0))],
        out_specs=pl.BlockSpec((1, s, hd), lambda i, h: (i, 0, h)),
        out_shape=jax.ShapeDtypeStruct((b, s, ATTN_HEADS * hd), BF16),
        scratch_shapes=[pltpu.VMEM((s, hd), F32), pltpu.VMEM((s, hd), F32), pltpu.VMEM((s, hd), BF16)],
        compiler_params=_cparams(("parallel", "parallel")),
        name="moba",
    )(qkv, qkv, qkv, cos_full, sin_signed)


def _rope_tables(s):
    half = ATTN_HEAD_DIM // 2
    inv = ROPE_THETA ** (-jnp.arange(half, dtype=F32) / half)
    ang = jnp.arange(s).astype(F32)[:, None] * inv[None, :]
    cos, sin = jnp.cos(ang), jnp.sin(ang)
    return jnp.concatenate([cos, cos], axis=-1), jnp.concatenate([-sin, sin], axis=-1)


def kernel(x, l0_norm_mix, l0_w_in, l0_mlstm_gate_bias, l0_mlstm_norm, l0_ssd_conv_w, l0_ssd_conv_b,
           l0_ssd_dt_bias, l0_ssd_a_log, l0_ssd_d, l0_ssd_norm, l0_w_out, l0_norm_ffn, l0_ffn_gate,
           l0_ffn_up, l0_ffn_down, l1_norm_mix, l1_w_qkv, l1_w_o, l1_norm_ffn, l1_ffn_gate, l1_ffn_up,
           l1_ffn_down, final_norm):
    b, s, d = x.shape
    m = b * s
    assert d == D_MODEL and s % CHUNK == 0 and s % MOBA_BLOCK == 0 and m % 1024 == 0
    xr = x.reshape(m, d)

    w_big = jnp.concatenate([l0_w_in[:, :6144], l0_w_in[:, 6160:12304]], axis=1).astype(BF16)
    w_gate = jnp.concatenate([l0_w_in[:, 6144:6160], l0_w_in[:, 12304:12336],
                              jnp.zeros((d, GATE_W - 48), F32)], axis=1).astype(BF16)
    proj = norm_matmul(xr, l0_norm_mix, w_big, tm=1024, tn=1024, name="in_proj").reshape(b, s, PROJ_W)
    gates_raw = norm_matmul(xr, l0_norm_mix, w_gate, tm=1024, tn=GATE_W, name="gate_proj").reshape(b, s, GATE_W)

    pad = jnp.zeros((GATE_W - 48,), F32)
    bias_row = jnp.concatenate([l0_mlstm_gate_bias, l0_ssd_dt_bias, pad]).reshape(1, GATE_W)
    alog_row = jnp.concatenate([jnp.zeros((16,), F32), l0_ssd_a_log, pad]).reshape(1, GATE_W)
    gcol, grow = gate_prep(gates_raw, bias_row, alog_row)

    hm = mlstm(proj, gcol, grow, l0_mlstm_norm)
    d_row = jnp.repeat(l0_ssd_d, SSD_HEADDIM).reshape(1, SSD_D_INNER)
    ys = ssd(proj, l0_ssd_conv_w, l0_ssd_conv_b.reshape(1, -1), gcol, grow, d_row,
             l0_ssd_norm.reshape(1, SSD_D_INNER))
    mixed = jnp.concatenate([hm, ys], axis=-1).reshape(m, 2 * D_MODEL)
    xr = matmul_residual(mixed, l0_w_out.astype(BF16), xr, tm=512, tn=1024, name="l0_out_proj")

    hff = swiglu_up(xr, l0_norm_ffn, l0_ffn_gate.astype(BF16), l0_ffn_up.astype(BF16),
                    tm=1024, tn=512, name="l0_ffn_up")
    xr = matmul_residual(hff, l0_ffn_down.astype(BF16), xr, tm=512, tn=1024, name="l0_ffn_down")

    qkv = norm_matmul(xr, l1_norm_mix, l1_w_qkv.astype(BF16), tm=1024, tn=1024, name="qkv_proj").reshape(b, s, 3 * D_MODEL)
    cos_full, sin_signed = _rope_tables(s)
    att = moba(qkv, cos_full, sin_signed).reshape(m, D_MODEL)
    xr = matmul_residual(att, l1_w_o.astype(BF16), xr, tm=512, tn=1024, name="l1_out_proj")

    hff = swiglu_up(xr, l1_norm_ffn, l1_ffn_gate.astype(BF16), l1_ffn_up.astype(BF16),
                    tm=1024, tn=512, name="l1_ffn_up")
    xr = matmul_residual(hff, l1_ffn_down.astype(BF16), xr, tm=512, tn=1024, name="l1_ffn_down")

    return rmsnorm(xr, final_norm, tm=512, name="final_norm").reshape(b, s, d)
```

```python
import functools

import jax
import jax.numpy as jnp
from jax import lax
from jax.experimental import pallas as pl
from jax.experimental.pallas import tpu as pltpu

F32 = jnp.float32
BF16 = jnp.bfloat16

D_MODEL = 2048
NORM_EPS = 1e-6
MLSTM_HEADS = 8
MLSTM_DQK = 128
MLSTM_DV = 256
SSD_D_INNER = 2048
SSD_HEADDIM = 64
SSD_HEADS = 32
SSD_STATE = 128
SSD_GROUPS = 8
SSD_HEADS_PER_GROUP = SSD_HEADS // SSD_GROUPS
SSD_GROUP_W = SSD_HEADS_PER_GROUP * SSD_HEADDIM
SSD_CONV = 4
ATTN_HEADS = 16
ATTN_HEAD_DIM = 128
MOBA_BLOCK = 256
MOBA_TOPK = 3
ROPE_THETA = 10000.0
LOG2E = 1.4426950408889634

IN_QKVO_END = 6144
IN_ZX_START, IN_ZX_END = 6160, 12304
IN_END = 12336
OFF_Q, OFF_K, OFF_V, OFF_O = 0, 1024, 2048, 4096
OFF_Z, OFF_X, OFF_B, OFF_C = 0, 2048, 4096, 5120
GATE_I, GATE_F, GATE_DT, GATE_CS = 0, 8, 16, 48
GATE_COLS = 48
GATE_W = 128

CHUNK = 256
CONV_HALO = 8

V7X_VMEM_LIMIT_BYTES = 56 * 1024 * 1024


def _cparams(sem):
    return pltpu.CompilerParams(dimension_semantics=sem, vmem_limit_bytes=V7X_VMEM_LIMIT_BYTES)


def _sigmoid(x):
    return 1.0 / (1.0 + jnp.exp(-x))


def _silu(x):
    return x * _sigmoid(x)


def _softplus(x):
    return jnp.maximum(x, 0.0) + jnp.log(1.0 + jnp.exp(-jnp.abs(x)))


def _rms(x, g):
    ms = jnp.mean(x * x, axis=-1, keepdims=True)
    return x * lax.rsqrt(ms + NORM_EPS) * g


def _dot(a, b):
    return jnp.dot(a, b, preferred_element_type=F32)


def _dot_nt(a, b):
    return lax.dot_general(a, b, (((1,), (1,)), ((), ())), preferred_element_type=F32)


def _norm_matmul_kernel(x_ref, g_ref, w_ref, o_ref, xn_ref):
    @pl.when(pl.program_id(1) == 0)
    def _():
        xn_ref[...] = _rms(x_ref[...], g_ref[...]).astype(BF16)

    o_ref[...] = _dot(xn_ref[...], w_ref[...]).astype(o_ref.dtype)


def norm_matmul(x, g, w, *, tm, tn, out_dtype=F32, name):
    m, k = x.shape
    n = w.shape[1]
    return pl.pallas_call(
        _norm_matmul_kernel,
        grid=(m // tm, n // tn),
        in_specs=[pl.BlockSpec((tm, k), lambda i, j: (i, 0)),
                  pl.BlockSpec((1, k), lambda i, j: (0, 0)),
                  pl.BlockSpec((k, tn), lambda i, j: (0, j))],
        out_specs=pl.BlockSpec((tm, tn), lambda i, j: (i, j)),
        out_shape=jax.ShapeDtypeStruct((m, n), out_dtype),
        scratch_shapes=[pltpu.VMEM((tm, k), BF16)],
        compiler_params=_cparams(("parallel", "arbitrary")),
        name=name,
    )(x, g.reshape(1, k), w)


def _swiglu_up_kernel(x_ref, g_ref, wg_ref, wu_ref, o_ref, xn_ref):
    @pl.when(pl.program_id(1) == 0)
    def _():
        xn_ref[...] = _rms(x_ref[...], g_ref[...]).astype(BF16)

    xn = xn_ref[...]
    gate = _dot(xn, wg_ref[...].astype(BF16))
    up = _dot(xn, wu_ref[...].astype(BF16))
    o_ref[...] = (_silu(gate) * up).astype(o_ref.dtype)


def swiglu_up(x, g, wg, wu, *, tm, tn, name):
    m, k = x.shape
    n = wg.shape[1]
    return pl.pallas_call(
        _swiglu_up_kernel,
        grid=(m // tm, n // tn),
        in_specs=[pl.BlockSpec((tm, k), lambda i, j: (i, 0)),
                  pl.BlockSpec((1, k), lambda i, j: (0, 0)),
                  pl.BlockSpec((k, tn), lambda i, j: (0, j)),
                  pl.BlockSpec((k, tn), lambda i, j: (0, j))],
        out_specs=pl.BlockSpec((tm, tn), lambda i, j: (i, j)),
        out_shape=jax.ShapeDtypeStruct((m, n), BF16),
        scratch_shapes=[pltpu.VMEM((tm, k), BF16)],
        compiler_params=_cparams(("parallel", "arbitrary")),
        name=name,
    )(x, g.reshape(1, k), wg, wu)


def _matmul_residual_kernel(a_ref, w_ref, r_ref, o_ref):
    o_ref[...] = r_ref[...] + _dot(a_ref[...], w_ref[...])


def matmul_residual(a, w, res, *, tm, tn, name):
    m, k = a.shape
    n = w.shape[1]
    return pl.pallas_call(
        _matmul_residual_kernel,
        grid=(m // tm, n // tn),
        in_specs=[pl.BlockSpec((tm, k), lambda i, j: (i, 0)),
                  pl.BlockSpec((k, tn), lambda i, j: (0, j)),
                  pl.BlockSpec((tm, tn), lambda i, j: (i, j))],
        out_specs=pl.BlockSpec((tm, tn), lambda i, j: (i, j)),
        out_shape=jax.ShapeDtypeStruct((m, n), F32),
        compiler_params=_cparams(("parallel", "parallel")),
        name=name,
    )(a, w, res)


def _matmul2_residual_kernel(a1_ref, a2_ref, w_ref, r_ref, o_ref):
    k1 = a1_ref.shape[1]
    acc = _dot(a1_ref[...], w_ref[pl.ds(0, k1), :])
    acc = acc + _dot(a2_ref[...], w_ref[pl.ds(k1, a2_ref.shape[1]), :])
    o_ref[...] = r_ref[...] + acc


def matmul2_residual(a1, a2, w, res, *, tm, tn, name):
    m, k1 = a1.shape
    k2 = a2.shape[1]
    n = w.shape[1]
    return pl.pallas_call(
        _matmul2_residual_kernel,
        grid=(m // tm, n // tn),
        in_specs=[pl.BlockSpec((tm, k1), lambda i, j: (i, 0)),
                  pl.BlockSpec((tm, k2), lambda i, j: (i, 0)),
                  pl.BlockSpec((k1 + k2, tn), lambda i, j: (0, j)),
                  pl.BlockSpec((tm, tn), lambda i, j: (i, j))],
        out_specs=pl.BlockSpec((tm, tn), lambda i, j: (i, j)),
        out_shape=jax.ShapeDtypeStruct((m, n), F32),
        compiler_params=_cparams(("parallel", "parallel")),
        name=name,
    )(a1, a2, w, res)


def _rmsnorm_kernel(x_ref, g_ref, o_ref):
    o_ref[...] = _rms(x_ref[...], g_ref[...])


def rmsnorm(x, g, *, tm, name):
    m, k = x.shape
    return pl.pallas_call(
        _rmsnorm_kernel,
        grid=(m // tm,),
        in_specs=[pl.BlockSpec((tm, k), lambda i: (i, 0)),
                  pl.BlockSpec((1, k), lambda i: (0, 0))],
        out_specs=pl.BlockSpec((tm, k), lambda i: (i, 0)),
        out_shape=jax.ShapeDtypeStruct((m, k), F32),
        compiler_params=_cparams(("parallel",)),
        name=name,
    )(x, g.reshape(1, k))


def _cumsum_rows(x):
    n = x.shape[0]
    row = lax.broadcasted_iota(jnp.int32, x.shape, 0)
    shift = 1
    while shift < n:
        x = x + jnp.where(row >= shift, pltpu.roll(x, shift, axis=0), 0.0)
        shift *= 2
    return x


def _gate_prep_kernel(raw_ref, bias_ref, alog_ref, col_ref, row_ref):
    v = raw_ref[0] + bias_ref[...]
    lane = lax.broadcasted_iota(jnp.int32, v.shape, 1)
    is_f = (lane >= GATE_F) & (lane < GATE_DT)
    is_dt = (lane >= GATE_DT) & (lane < GATE_CS)
    logf = -_softplus(-v)
    dt = _softplus(v)
    da = jnp.where(is_dt, dt * (-jnp.exp(alog_ref[...])), 0.0)
    pre = jnp.where(is_f, logf, 0.0) + pltpu.roll(da, GATE_CS - GATE_DT, axis=1)
    cum = _cumsum_rows(pre) * LOG2E
    out = jnp.where(lane < GATE_F, v * LOG2E, jnp.where(is_f, cum, jnp.where(is_dt, dt, cum)))
    col_ref[0] = out
    row_ref[0] = out.T


def gate_prep(raw, bias_row, alog_row):
    b, s, w = raw.shape
    return pl.pallas_call(
        _gate_prep_kernel,
        grid=(b, s // CHUNK),
        in_specs=[pl.BlockSpec((1, CHUNK, w), lambda i, c: (i, c, 0)),
                  pl.BlockSpec((1, w), lambda i, c: (0, 0)),
                  pl.BlockSpec((1, w), lambda i, c: (0, 0))],
        out_specs=[pl.BlockSpec((1, CHUNK, w), lambda i, c: (i, c, 0)),
                   pl.BlockSpec((1, w, CHUNK), lambda i, c: (i, 0, c))],
        out_shape=[jax.ShapeDtypeStruct((b, s, w), F32),
                   jax.ShapeDtypeStruct((b, w, s), F32)],
        compiler_params=_cparams(("parallel", "parallel")),
        name="gate_prep",
    )(raw, bias_row, alog_row)


def _lane_column(tile, idx):
    lane = lax.broadcasted_iota(jnp.int32, tile.shape, 1)
    return jnp.sum(jnp.where(lane == idx, tile, 0.0), axis=1, keepdims=True)


def _mlstm_kernel(q_ref, k_ref, v_ref, o_ref, gcol_ref, grow_ref, gain_ref, out_ref):
    h = pl.program_id(1)
    s_len = q_ref.shape[1]
    L = CHUNK
    scale = MLSTM_DQK ** -0.5
    ri = lax.broadcasted_iota(jnp.int32, (L, L), 0)
    ci = lax.broadcasted_iota(jnp.int32, (L, L), 1)
    causal = ci <= ri
    gain = gain_ref[0]

    c_state = jnp.zeros((MLSTM_DQK, MLSTM_DV), F32)
    n_state = jnp.zeros((1, MLSTM_DQK), F32)
    m_state = jnp.zeros((1, 1), F32)

    for c in range(s_len // L):
        rows = pl.ds(c * L, L)
        q = q_ref[0, rows, :] * scale
        k = k_ref[0, rows, :]
        qb = q.astype(BF16)
        kb = k.astype(BF16)
        vb = v_ref[0, rows, :].astype(BF16)
        gc = gcol_ref[0, rows, :]
        i_col = _lane_column(gc, GATE_I + h)
        b_col = _lane_column(gc, GATE_F + h)
        i_row = grow_ref[0, pl.ds(GATE_I + h, 1), rows]
        b_row = grow_ref[0, pl.ds(GATE_F + h, 1), rows]

        log_d = jnp.where(causal, b_col - (b_row - i_row), -jnp.inf)
        log_inter = b_col + m_state
        m_row = jnp.maximum(log_inter, jnp.max(log_d, axis=1, keepdims=True))
        dmat = jnp.exp2(log_d - m_row)
        scores = _dot_nt(qb, kb) * dmat
        inter = jnp.exp2(log_inter - m_row)
        num = _dot(scores.astype(BF16), vb) + inter * _dot(qb, c_state.astype(BF16))
        nq = jnp.sum(scores, axis=1, keepdims=True) + inter * jnp.sum(q * n_state, axis=1, keepdims=True)
        denom = jnp.maximum(jnp.abs(nq), jnp.exp2(-m_row))
        hm = num / denom
        y = _rms(hm, gain) * _sigmoid(o_ref[0, rows, :])
        out_ref[0, rows, :] = y.astype(out_ref.dtype)

        btot = b_col[L - 1:L, :]
        lw = btot - b_col + i_col
        m_new = jnp.maximum(btot + m_state, jnp.max(lw, axis=0, keepdims=True))
        decay = jnp.exp2(btot + m_state - m_new)
        wk = jnp.exp2(lw - m_new) * k
        c_state = decay * c_state + _dot(wk.T.astype(BF16), vb)
        n_state = decay * n_state + jnp.sum(wk, axis=0, keepdims=True)
        m_state = m_new


def mlstm(proj, gcol, grow, gain):
    b, s, _ = proj.shape
    qk_blk = lambda off: pl.BlockSpec((1, s, MLSTM_DQK), lambda i, h: (i, 0, off // MLSTM_DQK + h))
    v_blk = lambda off: pl.BlockSpec((1, s, MLSTM_DV), lambda i, h: (i, 0, off // MLSTM_DV + h))
    return pl.pallas_call(
        _mlstm_kernel,
        grid=(b, MLSTM_HEADS),
        in_specs=[qk_blk(OFF_Q), qk_blk(OFF_K), v_blk(OFF_V), v_blk(OFF_O),
                  pl.BlockSpec((1, s, GATE_W), lambda i, h: (i, 0, 0)),
                  pl.BlockSpec((1, GATE_W, s), lambda i, h: (i, 0, 0)),
                  pl.BlockSpec((1, 1, MLSTM_DV), lambda i, h: (h, 0, 0))],
        out_specs=pl.BlockSpec((1, s, MLSTM_DV), lambda i, h: (i, 0, h)),
        out_shape=jax.ShapeDtypeStruct((b, s, MLSTM_HEADS * MLSTM_DV), BF16),
        compiler_params=_cparams(("parallel", "parallel")),
        name="mlstm",
    )(proj, proj, proj, proj, gcol, grow, gain.reshape(MLSTM_HEADS, 1, MLSTM_DV))


def _conv_silu(ref, c, w_ref, b_ref):
    L = CHUNK
    cur = ref[0, pl.ds(c * L, L), :]
    if c == 0:
        halo = jnp.zeros((CONV_HALO, cur.shape[1]), F32)
    else:
        halo = ref[0, pl.ds(c * L - CONV_HALO, CONV_HALO), :]
    ext = jnp.concatenate([halo, cur], axis=0)
    w = w_ref[...]
    acc = b_ref[...] + w[SSD_CONV - 1:SSD_CONV, :] * cur
    for back in range(1, SSD_CONV):
        shifted = pltpu.roll(ext, back, axis=0)[CONV_HALO:, :]
        acc = acc + w[SSD_CONV - 1 - back:SSD_CONV - back, :] * shifted
    return _silu(acc)


def _per_head_lanes(vals, shape):
    lane = lax.broadcasted_iota(jnp.int32, shape, 1)
    out = jnp.broadcast_to(vals[SSD_HEADS_PER_GROUP - 1], shape)
    for r in range(SSD_HEADS_PER_GROUP - 2, -1, -1):
        out = jnp.where(lane < (r + 1) * SSD_HEADDIM, vals[r], out)
    return out


def _ssd_kernel(x_ref, b_ref, c_ref, z_ref, wx_ref, wb_ref, wc_ref, bx_ref, bb_ref, bc_ref,
                gcol_ref, grow_ref, d_ref, gain_ref, out_ref):
    g = pl.program_id(1)
    s_len = x_ref.shape[1]
    L = CHUNK
    R = SSD_HEADS_PER_GROUP
    ri = lax.broadcasted_iota(jnp.int32, (L, L), 0)
    ci = lax.broadcasted_iota(jnp.int32, (L, L), 1)
    causal = ci <= ri
    lane_head = lax.broadcasted_iota(jnp.int32, (L, SSD_GROUP_W), 1) // SSD_HEADDIM

    state = jnp.zeros((SSD_STATE, SSD_GROUP_W), F32)

    for c in range(s_len // L):
        rows = pl.ds(c * L, L)
        xs = _conv_silu(x_ref, c, wx_ref, bx_ref)
        bm = _conv_silu(b_ref, c, wb_ref, bb_ref)
        cm = _conv_silu(c_ref, c, wc_ref, bc_ref)
        bmb = bm.astype(BF16)
        cmb = cm.astype(BF16)
        gc = gcol_ref[0, rows, :]
        dt_cols = [_lane_column(gc, GATE_DT + R * g + r) for r in range(R)]
        cs_cols = [_lane_column(gc, GATE_CS + R * g + r) for r in range(R)]
        cs_rows = [grow_ref[0, pl.ds(GATE_CS + R * g + r, 1), rows] for r in range(R)]

        dt_full = _per_head_lanes(dt_cols, (L, SSD_GROUP_W))
        cs_full = _per_head_lanes(cs_cols, (L, SSD_GROUP_W))
        cs_end = _per_head_lanes([col[L - 1:L, :] for col in cs_cols], (1, SSD_GROUP_W))

        xd = xs * dt_full
        xdb = xd.astype(BF16)
        cb = _dot_nt(cmb, bmb)
        y = _dot(cmb, state.astype(BF16)) * jnp.exp2(cs_full)
        for r in range(R):
            lmat = jnp.where(causal, jnp.exp2(cs_cols[r] - cs_rows[r]), 0.0)
            xr = jnp.where(lane_head == r, xdb, jnp.zeros_like(xdb))
            y = y + _dot((cb * lmat).astype(BF16), xr)
        y = y + d_ref[...] * xs
        y = y * _silu(z_ref[0, rows, :])
        out_ref[0, rows, :] = _rms(y, gain_ref[...]).astype(out_ref.dtype)

        decay_end = jnp.exp2(cs_end - cs_full)
        state = jnp.exp2(cs_end) * state + _dot(bm.T.astype(BF16), (xd * decay_end).astype(BF16))


def ssd(proj, conv_w, conv_b, gcol, grow, d_row, gain_row):
    b, s, _ = proj.shape
    G = SSD_GROUPS
    wide = lambda off: pl.BlockSpec((1, s, SSD_GROUP_W), lambda i, g: (i, 0, off // SSD_GROUP_W + g))
    narrow = lambda off: pl.BlockSpec((1, s, SSD_STATE), lambda i, g: (i, 0, off // SSD_STATE + g))
    cw = lambda rows, width, off: pl.BlockSpec((rows, width), lambda i, g: (0, off // width + g))
    return pl.pallas_call(
        _ssd_kernel,
        grid=(b, G),
        in_specs=[wide(OFF_X), narrow(OFF_B), narrow(OFF_C), wide(OFF_Z),
                  cw(SSD_CONV, SSD_GROUP_W, 0), cw(SSD_CONV, SSD_STATE, 2048), cw(SSD_CONV, SSD_STATE, 3072),
                  cw(1, SSD_GROUP_W, 0), cw(1, SSD_STATE, 2048), cw(1, SSD_STATE, 3072),
                  pl.BlockSpec((1, s, GATE_W), lambda i, g: (i, 0, 0)),
                  pl.BlockSpec((1, GATE_W, s), lambda i, g: (i, 0, 0)),
                  cw(1, SSD_GROUP_W, 0), cw(1, SSD_GROUP_W, 0)],
        out_specs=pl.BlockSpec((1, s, SSD_GROUP_W), lambda i, g: (i, 0, g)),
        out_shape=jax.ShapeDtypeStruct((b, s, SSD_D_INNER), BF16),
        compiler_params=_cparams(("parallel", "parallel")),
        name="ssd",
    )(proj, proj, proj, proj, conv_w, conv_w, conv_w, conv_b, conv_b, conv_b, gcol, grow, d_row, gain_row)


def _moba_kernel(q_ref, k_ref, v_ref, cos_ref, sin_ref, out_ref, qs_ref, qb_ref, kb_ref, vt_ref, s_ref, pb_ref):
    s_len = q_ref.shape[1]
    BLK = MOBA_BLOCK
    nb = s_len // BLK
    nbp = -(-nb // 8) * 8
    half = ATTN_HEAD_DIM // 2
    scale = ATTN_HEAD_DIM ** -0.5
    cos = cos_ref[...]
    sin = sin_ref[...]

    def rope(x):
        return x * cos + pltpu.roll(x, half, axis=1) * sin

    q = rope(q_ref[0])
    k = rope(k_ref[0])
    qs_ref[...] = q
    qb_ref[...] = (q * (scale * LOG2E)).astype(BF16)
    kb_ref[...] = k.astype(BF16)
    vt_ref[...] = v_ref[0].T.astype(BF16)

    row8 = lax.broadcasted_iota(jnp.int32, (nbp, ATTN_HEAD_DIM), 0)
    k_mean = jnp.zeros((nbp, ATTN_HEAD_DIM), F32)
    for j in range(nb):
        k_mean = jnp.where(row8 == j, jnp.mean(k[j * BLK:(j + 1) * BLK, :], axis=0, keepdims=True), k_mean)

    SUB = 8
    GRP = BLK // SUB
    gi = lax.broadcasted_iota(jnp.int32, (GRP, SUB, BLK), 0)
    si = lax.broadcasted_iota(jnp.int32, (GRP, SUB, BLK), 1)
    ci = lax.broadcasted_iota(jnp.int32, (GRP, SUB, BLK), 2)
    causal_bias = jnp.where(gi * SUB + si <= ci, 0.0, -jnp.inf)
    blk_row = lax.broadcasted_iota(jnp.int32, (nbp, BLK), 0)

    def scores(i):
        qrows = pl.ds(i * BLK, BLK)
        keep = None
        if i > MOBA_TOPK:
            gate = lax.dot_general(k_mean, qs_ref[qrows, :], (((1,), (1,)), ((), ())),
                                   precision=lax.Precision.HIGHEST, preferred_element_type=F32)
            rank = jnp.zeros((nbp, BLK), F32)
            for m in range(i):
                gm = gate[m:m + 1, :]
                ahead = (gm > gate) | ((gm == gate) & (blk_row > m))
                rank = rank + jnp.where(ahead, 1.0, 0.0)
            keep = rank < float(MOBA_TOPK)
        qb = qb_ref[qrows, :]
        m8 = None
        for j in range(i + 1):
            krows = pl.ds(j * BLK, BLK)
            sj = _dot_nt(kb_ref[krows, :], qb).reshape(GRP, SUB, BLK)
            if j == i:
                sj = sj + causal_bias
            elif keep is not None:
                sj = sj + jnp.broadcast_to(jnp.where(keep[j:j + 1, :], 0.0, -jnp.inf), (SUB, BLK))[None]
            s_ref[i % 2, krows, :] = sj.reshape(BLK, BLK)
            mj = jnp.max(sj, axis=0)
            m8 = mj if m8 is None else jnp.maximum(m8, mj)
        return m8

    def attend(i, m8):
        qrows = pl.ds(i * BLK, BLK)
        nk = (i + 1) * BLK
        m_row = jnp.broadcast_to(jnp.max(m8, axis=0, keepdims=True), (SUB, BLK))
        den8 = jnp.zeros((SUB, BLK), F32)
        for j in range(i + 1):
            krows = pl.ds(j * BLK, BLK)
            p = jnp.exp2(s_ref[i % 2, krows, :].reshape(GRP, SUB, BLK) - m_row[None])
            den8 = den8 + jnp.sum(p, axis=0)
            pb_ref[i % 2, krows, :] = p.reshape(BLK, BLK).astype(BF16)
        den = jnp.sum(den8, axis=0, keepdims=True)
        ot = _dot(vt_ref[:, pl.ds(0, nk)], pb_ref[i % 2, pl.ds(0, nk), :])
        out_ref[0, qrows, :] = (ot / den).T.astype(out_ref.dtype)

    m8_next = scores(0)
    for i in range(nb):
        m8_cur = m8_next
        if i + 1 < nb:
            m8_next = scores(i + 1)
        attend(i, m8_cur)


def moba(qkv, cos_full, sin_signed):
    b, s, _ = qkv.shape
    hd = ATTN_HEAD_DIM
    blk = lambda part: pl.BlockSpec((1, s, hd), lambda i, h: (i, 0, part * ATTN_HEADS + h))
    return pl.pallas_call(
        _moba_kernel,
        grid=(b, ATTN_HEADS),
        in_specs=[blk(0), blk(1), blk(2),
                  pl.BlockSpec((s, hd), lambda i, h: (0, 0)),
                  pl.BlockSpec((s, hd), lambda i, h: (0, 0))],
        out_specs=pl.BlockSpec((1, s, hd), lambda i, h: (i, 0, h)),
        out_shape=jax.ShapeDtypeStruct((b, s, ATTN_HEADS * hd), BF16),
        scratch_shapes=[pltpu.VMEM((s, hd), F32), pltpu.VMEM((s, hd), BF16), pltpu.VMEM((s, hd), BF16),
                        pltpu.VMEM((hd, s), BF16), pltpu.VMEM((2, s, MOBA_BLOCK), F32),
                        pltpu.VMEM((2, s, MOBA_BLOCK), BF16)],
        compiler_params=_cparams(("parallel", "parallel")),
        name="moba",
    )(qkv, qkv, qkv, cos_full, sin_signed)


def _rope_tables(s):
    half = ATTN_HEAD_DIM // 2
    inv = ROPE_THETA ** (-jnp.arange(half, dtype=F32) / half)
    ang = jnp.arange(s).astype(F32)[:, None] * inv[None, :]
    cos, sin = jnp.cos(ang), jnp.sin(ang)
    return jnp.concatenate([cos, cos], axis=-1), jnp.concatenate([-sin, sin], axis=-1)


def kernel(x, l0_norm_mix, l0_w_in, l0_mlstm_gate_bias, l0_mlstm_norm, l0_ssd_conv_w, l0_ssd_conv_b,
           l0_ssd_dt_bias, l0_ssd_a_log, l0_ssd_d, l0_ssd_norm, l0_w_out, l0_norm_ffn, l0_ffn_gate,
           l0_ffn_up, l0_ffn_down, l1_norm_mix, l1_w_qkv, l1_w_o, l1_norm_ffn, l1_ffn_gate, l1_ffn_up,
           l1_ffn_down, final_norm):
    b, s, d = x.shape
    m = b * s
    assert d == D_MODEL and s % CHUNK == 0 and s % MOBA_BLOCK == 0 and m % 1024 == 0
    xr = x.reshape(m, d)

    w_qkvo = l0_w_in[:, :IN_QKVO_END].astype(BF16)
    w_zx = l0_w_in[:, IN_ZX_START:IN_ZX_END].astype(BF16)
    w_gate = jnp.concatenate([l0_w_in[:, IN_QKVO_END:IN_ZX_START], l0_w_in[:, IN_ZX_END:IN_END],
                              jnp.zeros((d, GATE_W - GATE_COLS), F32)], axis=1).astype(BF16)
    proj_a = norm_matmul(xr, l0_norm_mix, w_qkvo, tm=1024, tn=1024, name="in_proj_qkvo").reshape(b, s, -1)
    proj_b = norm_matmul(xr, l0_norm_mix, w_zx, tm=1024, tn=1024, name="in_proj_zx").reshape(b, s, -1)
    gates_raw = norm_matmul(xr, l0_norm_mix, w_gate, tm=1024, tn=GATE_W, name="gate_proj").reshape(b, s, GATE_W)

    pad = jnp.zeros((GATE_W - GATE_COLS,), F32)
    bias_row = jnp.concatenate([l0_mlstm_gate_bias, l0_ssd_dt_bias, pad]).reshape(1, GATE_W)
    alog_row = jnp.concatenate([jnp.zeros((GATE_DT,), F32), l0_ssd_a_log, pad]).reshape(1, GATE_W)
    gcol, grow = gate_prep(gates_raw, bias_row, alog_row)

    hm = mlstm(proj_a, gcol, grow, l0_mlstm_norm).reshape(m, D_MODEL)
    d_row = jnp.repeat(l0_ssd_d, SSD_HEADDIM).reshape(1, SSD_D_INNER)
    ys = ssd(proj_b, l0_ssd_conv_w, l0_ssd_conv_b.reshape(1, -1), gcol, grow, d_row,
             l0_ssd_norm.reshape(1, SSD_D_INNER)).reshape(m, SSD_D_INNER)
    xr = matmul2_residual(hm, ys, l0_w_out.astype(BF16), xr, tm=1024, tn=1024, name="l0_out_proj")

    hff = swiglu_up(xr, l0_norm_ffn, l0_ffn_gate, l0_ffn_up, tm=1024, tn=512, name="l0_ffn_up")
    xr = matmul_residual(hff, l0_ffn_down.astype(BF16), xr, tm=512, tn=1024, name="l0_ffn_down")

    qkv = norm_matmul(xr, l1_norm_mix, l1_w_qkv.astype(BF16), tm=1024, tn=1024, name="qkv_proj").reshape(b, s, 3 * D_MODEL)
    cos_full, sin_signed = _rope_tables(s)
    att = moba(qkv, cos_full, sin_signed).reshape(m, D_MODEL)
    xr = matmul_residual(att, l1_w_o.astype(BF16), xr, tm=1024, tn=1024, name="l1_out_proj")

    hff = swiglu_up(xr, l1_norm_ffn, l1_ffn_gate, l1_ffn_up, tm=1024, tn=512, name="l1_ffn_up")
    xr = matmul_residual(hff, l1_ffn_down.astype(BF16), xr, tm=512, tn=1024, name="l1_ffn_down")

    return rmsnorm(xr, final_norm, tm=512, name="final_norm").reshape(b, s, d)
```

```python
import functools

import jax
import jax.numpy as jnp
from jax import lax
from jax.experimental import pallas as pl
from jax.experimental.pallas import tpu as pltpu

F32 = jnp.float32
BF16 = jnp.bfloat16

D_MODEL = 2048
NORM_EPS = 1e-6
MLSTM_HEADS = 8
MLSTM_DQK = 128
MLSTM_DV = 256
SSD_D_INNER = 2048
SSD_HEADDIM = 64
SSD_HEADS = 32
SSD_STATE = 128
SSD_GROUPS = 8
SSD_HEADS_PER_GROUP = SSD_HEADS // SSD_GROUPS
SSD_GROUP_W = SSD_HEADS_PER_GROUP * SSD_HEADDIM
SSD_CONV = 4
ATTN_HEADS = 16
ATTN_HEAD_DIM = 128
MOBA_BLOCK = 256
MOBA_TOPK = 3
ROPE_THETA = 10000.0
LOG2E = 1.4426950408889634

IN_QKVO_END = 6144
IN_ZX_START, IN_ZX_END = 6160, 12304
IN_END = 12336
OFF_Q, OFF_K, OFF_V, OFF_O = 0, 1024, 2048, 4096
OFF_Z, OFF_X, OFF_B, OFF_C = 0, 2048, 4096, 5120
GATE_I, GATE_F, GATE_DT, GATE_CS = 0, 8, 16, 48
GATE_COLS = 48
GATE_W = 128

CHUNK = 256
CONV_HALO = 8
LANES = 128
WCAST_ROWS = 256

V7X_VMEM_LIMIT_BYTES = 56 * 1024 * 1024


def _cparams(sem):
    return pltpu.CompilerParams(dimension_semantics=sem, vmem_limit_bytes=V7X_VMEM_LIMIT_BYTES)


def _sigmoid(x):
    return 1.0 / (1.0 + jnp.exp(-x))


def _silu(x):
    return x * _sigmoid(x)


def _softplus(x):
    return jnp.maximum(x, 0.0) + jnp.log(1.0 + jnp.exp(-jnp.abs(x)))


def _rms(x, g):
    ms = jnp.mean(x * x, axis=-1, keepdims=True)
    return x * lax.rsqrt(ms + NORM_EPS) * g


def _dot(a, b):
    return jnp.dot(a, b, preferred_element_type=F32)


def _dot_nt(a, b):
    return lax.dot_general(a, b, (((1,), (1,)), ((), ())), preferred_element_type=F32)


def _swiglu_up_kernel(x_ref, wg_ref, wu_ref, o_ref, wg_scr, wu_scr):
    @pl.when(pl.program_id(1) == 0)
    def _():
        for r in range(0, wg_scr.shape[0], WCAST_ROWS):
            rows = pl.ds(r, WCAST_ROWS)
            wg_scr[rows, :] = wg_ref[rows, :].astype(BF16)
            wu_scr[rows, :] = wu_ref[rows, :].astype(BF16)

    x = x_ref[...]
    gate = _dot(x, wg_scr[...])
    up = _dot(x, wu_scr[...])
    o_ref[...] = (_silu(gate) * up).astype(o_ref.dtype)


def swiglu_up(xn, wg, wu, *, tm, tn, name):
    m, k = xn.shape
    n = wg.shape[1]
    return pl.pallas_call(
        _swiglu_up_kernel,
        grid=(n // tn, m // tm),
        in_specs=[pl.BlockSpec((tm, k), lambda j, i: (i, 0)),
                  pl.BlockSpec((k, tn), lambda j, i: (0, j)),
                  pl.BlockSpec((k, tn), lambda j, i: (0, j))],
        out_specs=pl.BlockSpec((tm, tn), lambda j, i: (i, j)),
        out_shape=jax.ShapeDtypeStruct((m, n), BF16),
        scratch_shapes=[pltpu.VMEM((k, tn), BF16), pltpu.VMEM((k, tn), BF16)],
        compiler_params=_cparams(("parallel", "arbitrary")),
        name=name,
    )(xn, wg, wu)


def _matmul_residual_norm_kernel(*refs, n_a, emit_x):
    a_refs = refs[:n_a]
    w_ref, r_ref, g_ref = refs[n_a:n_a + 3]
    outs = refs[n_a + 3:]
    acc = r_ref[...]
    off = 0
    for a_ref in a_refs:
        kk = a_ref.shape[1]
        acc = acc + _dot(a_ref[...], w_ref[pl.ds(off, kk), :])
        off += kk
    if emit_x:
        outs[0][...] = acc
    xn_ref = outs[-1]
    xn_ref[...] = _rms(acc, g_ref[...]).astype(xn_ref.dtype)


def matmul_residual_norm(a_list, w, res, g, *, tm, emit_x, norm_dtype, name):
    m, n = res.shape
    k = w.shape[0]
    assert sum(a.shape[1] for a in a_list) == k
    row_blk = lambda width: pl.BlockSpec((tm, width), lambda i: (i, 0))
    out_specs = [row_blk(n)]
    out_shape = [jax.ShapeDtypeStruct((m, n), norm_dtype)]
    if emit_x:
        out_specs = [row_blk(n)] + out_specs
        out_shape = [jax.ShapeDtypeStruct((m, n), F32)] + out_shape
    return pl.pallas_call(
        functools.partial(_matmul_residual_norm_kernel, n_a=len(a_list), emit_x=emit_x),
        grid=(m // tm,),
        in_specs=[row_blk(a.shape[1]) for a in a_list] + [
            pl.BlockSpec((k, n), lambda i: (0, 0), pipeline_mode=pl.Buffered(1)),
            row_blk(n),
            pl.BlockSpec((1, n), lambda i: (0, 0))],
        out_specs=out_specs,
        out_shape=out_shape,
        compiler_params=_cparams(("parallel",)),
        name=name,
    )(*a_list, w, res, g.reshape(1, n))


def _rmsnorm_kernel(x_ref, g_ref, o_ref):
    o_ref[...] = _rms(x_ref[...], g_ref[...]).astype(o_ref.dtype)


def rmsnorm(x, g, *, tm, out_dtype, name):
    m, k = x.shape
    return pl.pallas_call(
        _rmsnorm_kernel,
        grid=(m // tm,),
        in_specs=[pl.BlockSpec((tm, k), lambda i: (i, 0)),
                  pl.BlockSpec((1, k), lambda i: (0, 0))],
        out_specs=pl.BlockSpec((tm, k), lambda i: (i, 0)),
        out_shape=jax.ShapeDtypeStruct((m, k), out_dtype),
        compiler_params=_cparams(("parallel",)),
        name=name,
    )(x, g.reshape(1, k))


def _wres_matmul_kernel(x_ref, *refs, shift):
    if shift == 0:
        wa_ref, o_ref, w_scr = refs
    else:
        wa_ref, wb_ref, o_ref, w_scr = refs

    @pl.when(pl.program_id(1) == 0)
    def _():
        k, tn = w_scr.shape
        for r in range(0, k, WCAST_ROWS):
            rows = pl.ds(r, WCAST_ROWS)
            if shift == 0:
                w_scr[rows, :] = wa_ref[rows, :].astype(BF16)
            else:
                cat = jnp.concatenate([wa_ref[rows, :], wb_ref[rows, :]], axis=1)
                w_scr[rows, :] = pltpu.roll(cat, cat.shape[1] - shift, axis=1)[:, :tn].astype(BF16)

    o_ref[...] = _dot(x_ref[...], w_scr[...]).astype(o_ref.dtype)


def wres_matmul(x, w, col_start, n_out, *, tm, tn, out_dtype, name):
    m, k = x.shape
    shift = col_start % LANES
    base = col_start - shift
    assert base % tn == 0 and n_out % tn == 0 and k % WCAST_ROWS == 0
    assert col_start + n_out <= w.shape[1]
    w_specs = [pl.BlockSpec((k, tn), lambda j, i: (0, base // tn + j))]
    if shift:
        w_specs.append(pl.BlockSpec((k, LANES), lambda j, i: (0, (base + (j + 1) * tn) // LANES)))
    return pl.pallas_call(
        functools.partial(_wres_matmul_kernel, shift=shift),
        grid=(n_out // tn, m // tm),
        in_specs=[pl.BlockSpec((tm, k), lambda j, i: (i, 0))] + w_specs,
        out_specs=pl.BlockSpec((tm, tn), lambda j, i: (i, j)),
        out_shape=jax.ShapeDtypeStruct((m, n_out), out_dtype),
        scratch_shapes=[pltpu.VMEM((k, tn), BF16)],
        compiler_params=_cparams(("parallel", "arbitrary")),
        name=name,
    )(x, *([w] * len(w_specs)))


def _cumsum_rows(x):
    n = x.shape[0]
    row = lax.broadcasted_iota(jnp.int32, x.shape, 0)
    shift = 1
    while shift < n:
        x = x + jnp.where(row >= shift, pltpu.roll(x, shift, axis=0), 0.0)
        shift *= 2
    return x


def _gate_prep_kernel(xn_ref, w_ref, bias_ref, alog_ref, col_ref, row_ref):
    v = _dot(xn_ref[0], w_ref[...]) + bias_ref[...]
    lane = lax.broadcasted_iota(jnp.int32, v.shape, 1)
    is_f = (lane >= GATE_F) & (lane < GATE_DT)
    is_dt = (lane >= GATE_DT) & (lane < GATE_CS)
    logf = -_softplus(-v)
    dt = _softplus(v)
    da = jnp.where(is_dt, dt * (-jnp.exp(alog_ref[...])), 0.0)
    pre = jnp.where(is_f, logf, 0.0) + pltpu.roll(da, GATE_CS - GATE_DT, axis=1)
    cum = _cumsum_rows(pre) * LOG2E
    out = jnp.where(lane < GATE_F, v * LOG2E, jnp.where(is_f, cum, jnp.where(is_dt, dt, cum)))
    col_ref[0] = out
    row_ref[0] = out.T


def gate_prep(xn, w_gate, bias_row, alog_row):
    b, s, k = xn.shape
    w = w_gate.shape[1]
    return pl.pallas_call(
        _gate_prep_kernel,
        grid=(b, s // CHUNK),
        in_specs=[pl.BlockSpec((1, CHUNK, k), lambda i, c: (i, c, 0)),
                  pl.BlockSpec((k, w), lambda i, c: (0, 0)),
                  pl.BlockSpec((1, w), lambda i, c: (0, 0)),
                  pl.BlockSpec((1, w), lambda i, c: (0, 0))],
        out_specs=[pl.BlockSpec((1, CHUNK, w), lambda i, c: (i, c, 0)),
                   pl.BlockSpec((1, w, CHUNK), lambda i, c: (i, 0, c))],
        out_shape=[jax.ShapeDtypeStruct((b, s, w), F32),
                   jax.ShapeDtypeStruct((b, w, s), F32)],
        compiler_params=_cparams(("parallel", "parallel")),
        name="gate_prep",
    )(xn, w_gate, bias_row, alog_row)


def _lane_column(tile, idx):
    lane = lax.broadcasted_iota(jnp.int32, tile.shape, 1)
    return jnp.sum(jnp.where(lane == idx, tile, 0.0), axis=1, keepdims=True)


def _mlstm_kernel(q_ref, k_ref, v_ref, o_ref, gcol_ref, grow_ref, gain_ref, out_ref):
    h = pl.program_id(1)
    s_len = q_ref.shape[1]
    L = CHUNK
    scale = MLSTM_DQK ** -0.5
    ri = lax.broadcasted_iota(jnp.int32, (L, L), 0)
    ci = lax.broadcasted_iota(jnp.int32, (L, L), 1)
    causal = ci <= ri
    gain = gain_ref[0]

    c_state = jnp.zeros((MLSTM_DQK, MLSTM_DV), F32)
    n_state = jnp.zeros((1, MLSTM_DQK), F32)
    m_state = jnp.zeros((1, 1), F32)

    for c in range(s_len // L):
        rows = pl.ds(c * L, L)
        q = q_ref[0, rows, :] * scale
        k = k_ref[0, rows, :]
        qb = q.astype(BF16)
        kb = k.astype(BF16)
        vb = v_ref[0, rows, :].astype(BF16)
        gc = gcol_ref[0, rows, :]
        i_col = _lane_column(gc, GATE_I + h)
        b_col = _lane_column(gc, GATE_F + h)
        i_row = grow_ref[0, pl.ds(GATE_I + h, 1), rows]
        b_row = grow_ref[0, pl.ds(GATE_F + h, 1), rows]

        log_d = jnp.where(causal, b_col - (b_row - i_row), -jnp.inf)
        log_inter = b_col + m_state
        m_row = jnp.maximum(log_inter, jnp.max(log_d, axis=1, keepdims=True))
        dmat = jnp.exp2(log_d - m_row)
        scores = _dot_nt(qb, kb) * dmat
        inter = jnp.exp2(log_inter - m_row)
        num = _dot(scores.astype(BF16), vb) + inter * _dot(qb, c_state.astype(BF16))
        nq = jnp.sum(scores, axis=1, keepdims=True) + inter * jnp.sum(q * n_state, axis=1, keepdims=True)
        denom = jnp.maximum(jnp.abs(nq), jnp.exp2(-m_row))
        hm = num / denom
        y = _rms(hm, gain) * _sigmoid(o_ref[0, rows, :])
        out_ref[0, rows, :] = y.astype(out_ref.dtype)

        btot = b_col[L - 1:L, :]
        lw = btot - b_col + i_col
        m_new = jnp.maximum(btot + m_state, jnp.max(lw, axis=0, keepdims=True))
        decay = jnp.exp2(btot + m_state - m_new)
        wk = jnp.exp2(lw - m_new) * k
        c_state = decay * c_state + _dot(wk.T.astype(BF16), vb)
        n_state = decay * n_state + jnp.sum(wk, axis=0, keepdims=True)
        m_state = m_new


def mlstm(proj, gcol, grow, gain):
    b, s, _ = proj.shape
    qk_blk = lambda off: pl.BlockSpec((1, s, MLSTM_DQK), lambda i, h: (i, 0, off // MLSTM_DQK + h))
    v_blk = lambda off: pl.BlockSpec((1, s, MLSTM_DV), lambda i, h: (i, 0, off // MLSTM_DV + h))
    return pl.pallas_call(
        _mlstm_kernel,
        grid=(b, MLSTM_HEADS),
        in_specs=[qk_blk(OFF_Q), qk_blk(OFF_K), v_blk(OFF_V), v_blk(OFF_O),
                  pl.BlockSpec((1, s, GATE_W), lambda i, h: (i, 0, 0)),
                  pl.BlockSpec((1, GATE_W, s), lambda i, h: (i, 0, 0)),
                  pl.BlockSpec((1, 1, MLSTM_DV), lambda i, h: (h, 0, 0))],
        out_specs=pl.BlockSpec((1, s, MLSTM_DV), lambda i, h: (i, 0, h)),
        out_shape=jax.ShapeDtypeStruct((b, s, MLSTM_HEADS * MLSTM_DV), BF16),
        compiler_params=_cparams(("parallel", "parallel")),
        name="mlstm",
    )(proj, proj, proj, proj, gcol, grow, gain.reshape(MLSTM_HEADS, 1, MLSTM_DV))


def _conv_silu(ref, c, w_ref, b_ref):
    L = CHUNK
    cur = ref[0, pl.ds(c * L, L), :]
    if c == 0:
        halo = jnp.zeros((CONV_HALO, cur.shape[1]), F32)
    else:
        halo = ref[0, pl.ds(c * L - CONV_HALO, CONV_HALO), :]
    ext = jnp.concatenate([halo, cur], axis=0)
    w = w_ref[...]
    acc = b_ref[...] + w[SSD_CONV - 1:SSD_CONV, :] * cur
    for back in range(1, SSD_CONV):
        shifted = pltpu.roll(ext, back, axis=0)[CONV_HALO:, :]
        acc = acc + w[SSD_CONV - 1 - back:SSD_CONV - back, :] * shifted
    return _silu(acc)


def _per_head_lanes(vals, shape):
    lane = lax.broadcasted_iota(jnp.int32, shape, 1)
    out = jnp.broadcast_to(vals[SSD_HEADS_PER_GROUP - 1], shape)
    for r in range(SSD_HEADS_PER_GROUP - 2, -1, -1):
        out = jnp.where(lane < (r + 1) * SSD_HEADDIM, vals[r], out)
    return out


def _ssd_kernel(x_ref, b_ref, c_ref, z_ref, wx_ref, wb_ref, wc_ref, bx_ref, bb_ref, bc_ref,
                gcol_ref, grow_ref, d_ref, gain_ref, out_ref):
    g = pl.program_id(1)
    s_len = x_ref.shape[1]
    L = CHUNK
    R = SSD_HEADS_PER_GROUP
    ri = lax.broadcasted_iota(jnp.int32, (L, L), 0)
    ci = lax.broadcasted_iota(jnp.int32, (L, L), 1)
    causal = ci <= ri
    lane_head = lax.broadcasted_iota(jnp.int32, (L, SSD_GROUP_W), 1) // SSD_HEADDIM

    state = jnp.zeros((SSD_STATE, SSD_GROUP_W), F32)

    for c in range(s_len // L):
        rows = pl.ds(c * L, L)
        xs = _conv_silu(x_ref, c, wx_ref, bx_ref)
        bm = _conv_silu(b_ref, c, wb_ref, bb_ref)
        cm = _conv_silu(c_ref, c, wc_ref, bc_ref)
        bmb = bm.astype(BF16)
        cmb = cm.astype(BF16)
        gc = gcol_ref[0, rows, :]
        dt_cols = [_lane_column(gc, GATE_DT + R * g + r) for r in range(R)]
        cs_cols = [_lane_column(gc, GATE_CS + R * g + r) for r in range(R)]
        cs_rows = [grow_ref[0, pl.ds(GATE_CS + R * g + r, 1), rows] for r in range(R)]

        dt_full = _per_head_lanes(dt_cols, (L, SSD_GROUP_W))
        cs_full = _per_head_lanes(cs_cols, (L, SSD_GROUP_W))
        cs_end = _per_head_lanes([col[L - 1:L, :] for col in cs_cols], (1, SSD_GROUP_W))

        xd = xs * dt_full
        xdb = xd.astype(BF16)
        cb = _dot_nt(cmb, bmb)
        y = _dot(cmb, state.astype(BF16)) * jnp.exp2(cs_full)
        for r in range(R):
            lmat = jnp.where(causal, jnp.exp2(cs_cols[r] - cs_rows[r]), 0.0)
            xr = jnp.where(lane_head == r, xdb, jnp.zeros_like(xdb))
            y = y + _dot((cb * lmat).astype(BF16), xr)
        y = y + d_ref[...] * xs
        y = y * _silu(z_ref[0, rows, :])
        out_ref[0, rows, :] = _rms(y, gain_ref[...]).astype(out_ref.dtype)

        decay_end = jnp.exp2(cs_end - cs_full)
        state = jnp.exp2(cs_end) * state + _dot(bm.T.astype(BF16), (xd * decay_end).astype(BF16))


def ssd(proj, conv_w, conv_b, gcol, grow, d_row, gain_row):
    b, s, _ = proj.shape
    G = SSD_GROUPS
    wide = lambda off: pl.BlockSpec((1, s, SSD_GROUP_W), lambda i, g: (i, 0, off // SSD_GROUP_W + g))
    narrow = lambda off: pl.BlockSpec((1, s, SSD_STATE), lambda i, g: (i, 0, off // SSD_STATE + g))
    cw = lambda rows, width, off: pl.BlockSpec((rows, width), lambda i, g: (0, off // width + g))
    return pl.pallas_call(
        _ssd_kernel,
        grid=(b, G),
        in_specs=[wide(OFF_X), narrow(OFF_B), narrow(OFF_C), wide(OFF_Z),
                  cw(SSD_CONV, SSD_GROUP_W, 0), cw(SSD_CONV, SSD_STATE, 2048), cw(SSD_CONV, SSD_STATE, 3072),
                  cw(1, SSD_GROUP_W, 0), cw(1, SSD_STATE, 2048), cw(1, SSD_STATE, 3072),
                  pl.BlockSpec((1, s, GATE_W), lambda i, g: (i, 0, 0)),
                  pl.BlockSpec((1, GATE_W, s), lambda i, g: (i, 0, 0)),
                  cw(1, SSD_GROUP_W, 0), cw(1, SSD_GROUP_W, 0)],
        out_specs=pl.BlockSpec((1, s, SSD_GROUP_W), lambda i, g: (i, 0, g)),
        out_shape=jax.ShapeDtypeStruct((b, s, SSD_D_INNER), BF16),
        compiler_params=_cparams(("parallel", "parallel")),
        name="ssd",
    )(proj, proj, proj, proj, conv_w, conv_w, conv_w, conv_b, conv_b, conv_b, gcol, grow, d_row, gain_row)


def _moba_kernel(q_ref, k_ref, v_ref, cos_ref, sin_ref, out_ref, qs_ref, qb_ref, kb_ref, vt_ref, s_ref, pb_ref):
    s_len = q_ref.shape[1]
    BLK = MOBA_BLOCK
    nb = s_len // BLK
    nbp = -(-nb // 8) * 8
    half = ATTN_HEAD_DIM // 2
    scale = ATTN_HEAD_DIM ** -0.5
    cos = cos_ref[...]
    sin = sin_ref[...]

    def rope(x):
        return x * cos + pltpu.roll(x, half, axis=1) * sin

    q = rope(q_ref[0])
    k = rope(k_ref[0])
    qs_ref[...] = q
    qb_ref[...] = (q * (scale * LOG2E)).astype(BF16)
    kb_ref[...] = k.astype(BF16)
    vt_ref[...] = v_ref[0].T.astype(BF16)

    row8 = lax.broadcasted_iota(jnp.int32, (nbp, ATTN_HEAD_DIM), 0)
    k_mean = jnp.zeros((nbp, ATTN_HEAD_DIM), F32)
    for j in range(nb):
        k_mean = jnp.where(row8 == j, jnp.mean(k[j * BLK:(j + 1) * BLK, :], axis=0, keepdims=True), k_mean)

    SUB = 8
    GRP = BLK // SUB
    gi = lax.broadcasted_iota(jnp.int32, (GRP, SUB, BLK), 0)
    si = lax.broadcasted_iota(jnp.int32, (GRP, SUB, BLK), 1)
    ci = lax.broadcasted_iota(jnp.int32, (GRP, SUB, BLK), 2)
    causal_bias = jnp.where(gi * SUB + si <= ci, 0.0, -jnp.inf)
    blk_row = lax.broadcasted_iota(jnp.int32, (nbp, BLK), 0)

    def scores(i):
        qrows = pl.ds(i * BLK, BLK)
        keep = None
        if i > MOBA_TOPK:
            gate = lax.dot_general(k_mean, qs_ref[qrows, :], (((1,), (1,)), ((), ())),
                                   precision=lax.Precision.HIGHEST, preferred_element_type=F32)
            rank = jnp.zeros((nbp, BLK), F32)
            for m in range(i):
                gm = gate[m:m + 1, :]
                ahead = (gm > gate) | ((gm == gate) & (blk_row > m))
                rank = rank + jnp.where(ahead, 1.0, 0.0)
            keep = rank < float(MOBA_TOPK)
        qb = qb_ref[qrows, :]
        m8 = None
        for j in range(i + 1):
            krows = pl.ds(j * BLK, BLK)
            sj = _dot_nt(kb_ref[krows, :], qb).reshape(GRP, SUB, BLK)
            if j == i:
                sj = sj + causal_bias
            elif keep is not None:
                sj = sj + jnp.broadcast_to(jnp.where(keep[j:j + 1, :], 0.0, -jnp.inf), (SUB, BLK))[None]
            s_ref[i % 2, krows, :] = sj.reshape(BLK, BLK)
            mj = jnp.max(sj, axis=0)
            m8 = mj if m8 is None else jnp.maximum(m8, mj)
        return m8

    def attend(i, m8):
        qrows = pl.ds(i * BLK, BLK)
        nk = (i + 1) * BLK
        m_row = jnp.broadcast_to(jnp.max(m8, axis=0, keepdims=True), (SUB, BLK))
        den8 = jnp.zeros((SUB, BLK), F32)
        for j in range(i + 1):
            krows = pl.ds(j * BLK, BLK)
            p = jnp.exp2(s_ref[i % 2, krows, :].reshape(GRP, SUB, BLK) - m_row[None])
            den8 = den8 + jnp.sum(p, axis=0)
            pb_ref[i % 2, krows, :] = p.reshape(BLK, BLK).astype(BF16)
        den = jnp.sum(den8, axis=0, keepdims=True)
        ot = _dot(vt_ref[:, pl.ds(0, nk)], pb_ref[i % 2, pl.ds(0, nk), :])
        out_ref[0, qrows, :] = (ot / den).T.astype(out_ref.dtype)

    m8_next = scores(0)
    for i in range(nb):
        m8_cur = m8_next
        if i + 1 < nb:
            m8_next = scores(i + 1)
        attend(i, m8_cur)


def moba(qkv, cos_full, sin_signed):
    b, s, _ = qkv.shape
    hd = ATTN_HEAD_DIM
    blk = lambda part: pl.BlockSpec((1, s, hd), lambda i, h: (i, 0, part * ATTN_HEADS + h))
    return pl.pallas_call(
        _moba_kernel,
        grid=(b, ATTN_HEADS),
        in_specs=[blk(0), blk(1), blk(2),
                  pl.BlockSpec((s, hd), lambda i, h: (0, 0)),
                  pl.BlockSpec((s, hd), lambda i, h: (0, 0))],
        out_specs=pl.BlockSpec((1, s, hd), lambda i, h: (i, 0, h)),
        out_shape=jax.ShapeDtypeStruct((b, s, ATTN_HEADS * hd), BF16),
        scratch_shapes=[pltpu.VMEM((s, hd), F32), pltpu.VMEM((s, hd), BF16), pltpu.VMEM((s, hd), BF16),
                        pltpu.VMEM((hd, s), BF16), pltpu.VMEM((2, s, MOBA_BLOCK), F32),
                        pltpu.VMEM((2, s, MOBA_BLOCK), BF16)],
        compiler_params=_cparams(("parallel", "parallel")),
        name="moba",
    )(qkv, qkv, qkv, cos_full, sin_signed)


def _rope_tables(s):
    half = ATTN_HEAD_DIM // 2
    inv = ROPE_THETA ** (-jnp.arange(half, dtype=F32) / half)
    ang = jnp.arange(s).astype(F32)[:, None] * inv[None, :]
    cos, sin = jnp.cos(ang), jnp.sin(ang)
    return jnp.concatenate([cos, cos], axis=-1), jnp.concatenate([-sin, sin], axis=-1)


def kernel(x, l0_norm_mix, l0_w_in, l0_mlstm_gate_bias, l0_mlstm_norm, l0_ssd_conv_w, l0_ssd_conv_b,
           l0_ssd_dt_bias, l0_ssd_a_log, l0_ssd_d, l0_ssd_norm, l0_w_out, l0_norm_ffn, l0_ffn_gate,
           l0_ffn_up, l0_ffn_down, l1_norm_mix, l1_w_qkv, l1_w_o, l1_norm_ffn, l1_ffn_gate, l1_ffn_up,
           l1_ffn_down, final_norm):
    b, s, d = x.shape
    m = b * s
    assert d == D_MODEL and s % CHUNK == 0 and s % MOBA_BLOCK == 0 and m % 1024 == 0
    xr = x.reshape(m, d)

    xn = rmsnorm(xr, l0_norm_mix, tm=512, out_dtype=BF16, name="l0_norm")
    proj_a = wres_matmul(xn, l0_w_in, 0, IN_QKVO_END, tm=1024, tn=1024, out_dtype=F32,
                         name="in_proj_qkvo").reshape(b, s, -1)
    proj_b = wres_matmul(xn, l0_w_in, IN_ZX_START, IN_ZX_END - IN_ZX_START, tm=1024, tn=1024, out_dtype=F32,
                         name="in_proj_zx").reshape(b, s, -1)
    w_gate = jnp.concatenate([l0_w_in[:, IN_QKVO_END:IN_ZX_START], l0_w_in[:, IN_ZX_END:IN_END],
                              jnp.zeros((d, GATE_W - GATE_COLS), F32)], axis=1).astype(BF16)
    pad = jnp.zeros((GATE_W - GATE_COLS,), F32)
    bias_row = jnp.concatenate([l0_mlstm_gate_bias, l0_ssd_dt_bias, pad]).reshape(1, GATE_W)
    alog_row = jnp.concatenate([jnp.zeros((GATE_DT,), F32), l0_ssd_a_log, pad]).reshape(1, GATE_W)
    gcol, grow = gate_prep(xn.reshape(b, s, d), w_gate, bias_row, alog_row)

    hm = mlstm(proj_a, gcol, grow, l0_mlstm_norm).reshape(m, D_MODEL)
    d_row = jnp.repeat(l0_ssd_d, SSD_HEADDIM).reshape(1, SSD_D_INNER)
    ys = ssd(proj_b, l0_ssd_conv_w, l0_ssd_conv_b.reshape(1, -1), gcol, grow, d_row,
             l0_ssd_norm.reshape(1, SSD_D_INNER)).reshape(m, SSD_D_INNER)
    xr, xn = matmul_residual_norm([hm, ys], l0_w_out.astype(BF16), xr, l0_norm_ffn, tm=512, emit_x=True,
                                  norm_dtype=BF16, name="l0_out_proj")
    hff = swiglu_up(xn, l0_ffn_gate, l0_ffn_up, tm=1024, tn=512, name="l0_ffn_up")
    xr, xn = matmul_residual_norm([hff], l0_ffn_down.astype(BF16), xr, l1_norm_mix, tm=256, emit_x=True,
                                  norm_dtype=BF16, name="l0_ffn_down")

    qkv = wres_matmul(xn, l1_w_qkv, 0, 3 * D_MODEL, tm=1024, tn=1024, out_dtype=F32,
                      name="qkv_proj").reshape(b, s, 3 * D_MODEL)
    cos_full, sin_signed = _rope_tables(s)
    att = moba(qkv, cos_full, sin_signed).reshape(m, D_MODEL)
    xr, xn = matmul_residual_norm([att], l1_w_o.astype(BF16), xr, l1_norm_ffn, tm=512, emit_x=True,
                                  norm_dtype=BF16, name="l1_out_proj")
    hff = swiglu_up(xn, l1_ffn_gate, l1_ffn_up, tm=1024, tn=512, name="l1_ffn_up")
    (out,) = matmul_residual_norm([hff], l1_ffn_down.astype(BF16), xr, final_norm, tm=256, emit_x=False,
                                  norm_dtype=F32, name="l1_ffn_down")
    return out.reshape(b, s, d)
```

```python
import functools

import jax
import jax.numpy as jnp
from jax import lax
from jax.experimental import pallas as pl
from jax.experimental.pallas import tpu as pltpu

F32 = jnp.float32
BF16 = jnp.bfloat16

D_MODEL = 2048
NORM_EPS = 1e-6
MLSTM_HEADS = 8
MLSTM_DQK = 128
MLSTM_DV = 256
SSD_D_INNER = 2048
SSD_HEADDIM = 64
SSD_HEADS = 32
SSD_STATE = 128
SSD_GROUPS = 8
SSD_HEADS_PER_GROUP = SSD_HEADS // SSD_GROUPS
SSD_GROUP_W = SSD_HEADS_PER_GROUP * SSD_HEADDIM
SSD_CONV = 4
ATTN_HEADS = 16
ATTN_HEAD_DIM = 128
MOBA_BLOCK = 256
MOBA_TOPK = 3
ROPE_THETA = 10000.0
LOG2E = 1.4426950408889634

IN_QKVO_END = 6144
IN_ZX_START, IN_ZX_END = 6160, 12304
IN_END = 12336
OFF_Q, OFF_K, OFF_V, OFF_O = 0, 1024, 2048, 4096
OFF_Z, OFF_X, OFF_B, OFF_C = 0, 2048, 4096, 5120
GATE_I, GATE_F, GATE_DT, GATE_CS = 0, 8, 16, 48
GATE_COLS = 48
GATE_W = 128

CHUNK = 256
CONV_HALO = 8
BF16_TILE_ROWS = 16
TAIL_ROWS = 128
WCAST_ROWS = 256

V7X_VMEM_LIMIT_BYTES = 56 * 1024 * 1024


def _cparams(sem):
    return pltpu.CompilerParams(dimension_semantics=sem, vmem_limit_bytes=V7X_VMEM_LIMIT_BYTES)


def _sigmoid(x):
    return 1.0 / (1.0 + jnp.exp(-x))


def _silu(x):
    return x * _sigmoid(x)


def _softplus(x):
    return jnp.maximum(x, 0.0) + jnp.log(1.0 + jnp.exp(-jnp.abs(x)))


def _rms(x, g):
    ms = jnp.mean(x * x, axis=-1, keepdims=True)
    return x * lax.rsqrt(ms + NORM_EPS) * g


def _dot(a, b):
    return jnp.dot(a, b, preferred_element_type=F32)


def _dot_nt(a, b):
    return lax.dot_general(a, b, (((1,), (1,)), ((), ())), preferred_element_type=F32)


def _swiglu_up_kernel(x_ref, wg_ref, wu_ref, o_ref, wg_scr, wu_scr):
    @pl.when(pl.program_id(1) == 0)
    def _():
        for r in range(0, wg_scr.shape[0], WCAST_ROWS):
            rows = pl.ds(r, WCAST_ROWS)
            wg_scr[rows, :] = wg_ref[rows, :].astype(BF16)
            wu_scr[rows, :] = wu_ref[rows, :].astype(BF16)

    x = x_ref[...]
    gate = _dot(x, wg_scr[...])
    up = _dot(x, wu_scr[...])
    o_ref[...] = (_silu(gate) * up).astype(o_ref.dtype)


def swiglu_up(xn, wg, wu, *, tm, tn, name):
    m, k = xn.shape
    n = wg.shape[1]
    return pl.pallas_call(
        _swiglu_up_kernel,
        grid=(n // tn, m // tm),
        in_specs=[pl.BlockSpec((tm, k), lambda j, i: (i, 0)),
                  pl.BlockSpec((k, tn), lambda j, i: (0, j)),
                  pl.BlockSpec((k, tn), lambda j, i: (0, j))],
        out_specs=pl.BlockSpec((tm, tn), lambda j, i: (i, j)),
        out_shape=jax.ShapeDtypeStruct((m, n), BF16),
        scratch_shapes=[pltpu.VMEM((k, tn), BF16), pltpu.VMEM((k, tn), BF16)],
        compiler_params=_cparams(("parallel", "arbitrary")),
        name=name,
    )(xn, wg, wu)


def _matmul_residual_norm_kernel(*refs, n_a, emit_x):
    a_refs = refs[:n_a]
    w_ref, r_ref, g_ref = refs[n_a:n_a + 3]
    outs = refs[n_a + 3:]
    acc = r_ref[...]
    off = 0
    for a_ref in a_refs:
        kk = a_ref.shape[1]
        acc = acc + _dot(a_ref[...], w_ref[pl.ds(off, kk), :])
        off += kk
    if emit_x:
        outs[0][...] = acc
    xn_ref = outs[-1]
    xn_ref[...] = _rms(acc, g_ref[...]).astype(xn_ref.dtype)


def matmul_residual_norm(a_list, w, res, g, *, tm, emit_x, norm_dtype, name):
    m, n = res.shape
    k = w.shape[0]
    assert sum(a.shape[1] for a in a_list) == k
    row_blk = lambda width: pl.BlockSpec((tm, width), lambda i: (i, 0))
    out_specs = [row_blk(n)]
    out_shape = [jax.ShapeDtypeStruct((m, n), norm_dtype)]
    if emit_x:
        out_specs = [row_blk(n)] + out_specs
        out_shape = [jax.ShapeDtypeStruct((m, n), F32)] + out_shape
    return pl.pallas_call(
        functools.partial(_matmul_residual_norm_kernel, n_a=len(a_list), emit_x=emit_x),
        grid=(m // tm,),
        in_specs=[row_blk(a.shape[1]) for a in a_list] + [
            pl.BlockSpec((k, n), lambda i: (0, 0), pipeline_mode=pl.Buffered(1)),
            row_blk(n),
            pl.BlockSpec((1, n), lambda i: (0, 0))],
        out_specs=out_specs,
        out_shape=out_shape,
        compiler_params=_cparams(("parallel",)),
        name=name,
    )(*a_list, w, res, g.reshape(1, n))


def _rmsnorm_kernel(x_ref, g_ref, o_ref):
    o_ref[...] = _rms(x_ref[...], g_ref[...]).astype(o_ref.dtype)


def rmsnorm(x, g, *, tm, out_dtype, name):
    m, k = x.shape
    return pl.pallas_call(
        _rmsnorm_kernel,
        grid=(m // tm,),
        in_specs=[pl.BlockSpec((tm, k), lambda i: (i, 0)),
                  pl.BlockSpec((1, k), lambda i: (0, 0))],
        out_specs=pl.BlockSpec((tm, k), lambda i: (i, 0)),
        out_shape=jax.ShapeDtypeStruct((m, k), out_dtype),
        compiler_params=_cparams(("parallel",)),
        name=name,
    )(x, g.reshape(1, k))


def _wres_matmul_kernel(x_ref, *refs, shift, w_transposed):
    if shift == 0:
        wa_ref, o_ref, w_scr = refs
    else:
        wa_ref, wb_ref, o_ref, w_scr = refs

    @pl.when(pl.program_id(1) == 0)
    def _():
        if shift == 0:
            w_scr[...] = wa_ref[...].astype(BF16)
        else:
            tn = w_scr.shape[0]
            w_scr[pl.ds(0, tn - shift), :] = wa_ref[pl.ds(shift, tn - shift), :].astype(BF16)
            w_scr[pl.ds(tn - shift, shift), :] = wb_ref[pl.ds(0, shift), :].astype(BF16)

    mm = _dot_nt if w_transposed else _dot
    o_ref[...] = mm(x_ref[...], w_scr[...]).astype(o_ref.dtype)


def wres_matmul(x, w, start, n_out, *, w_transposed, tm, tn, out_dtype, name):
    m, k = x.shape
    shift = start % tn
    base = start - shift
    assert n_out % tn == 0 and (shift == 0 or (w_transposed and shift % BF16_TILE_ROWS == 0 and shift <= TAIL_ROWS))
    assert start + n_out <= w.shape[0 if w_transposed else 1]
    if w_transposed:
        w_specs = [pl.BlockSpec((tn, k), lambda j, i: (base // tn + j, 0))]
        if shift:
            w_specs.append(pl.BlockSpec((TAIL_ROWS, k), lambda j, i: ((base + (j + 1) * tn) // TAIL_ROWS, 0)))
        scratch = pltpu.VMEM((tn, k), BF16)
    else:
        w_specs = [pl.BlockSpec((k, tn), lambda j, i: (0, base // tn + j))]
        scratch = pltpu.VMEM((k, tn), BF16)
    return pl.pallas_call(
        functools.partial(_wres_matmul_kernel, shift=shift, w_transposed=w_transposed),
        grid=(n_out // tn, m // tm),
        in_specs=[pl.BlockSpec((tm, k), lambda j, i: (i, 0))] + w_specs,
        out_specs=pl.BlockSpec((tm, tn), lambda j, i: (i, j)),
        out_shape=jax.ShapeDtypeStruct((m, n_out), out_dtype),
        scratch_shapes=[scratch],
        compiler_params=_cparams(("parallel", "arbitrary")),
        name=name,
    )(x, *([w] * len(w_specs)))


def _cumsum_rows(x):
    n = x.shape[0]
    row = lax.broadcasted_iota(jnp.int32, x.shape, 0)
    shift = 1
    while shift < n:
        x = x + jnp.where(row >= shift, pltpu.roll(x, shift, axis=0), 0.0)
        shift *= 2
    return x


def _gate_prep_kernel(xn_ref, w_ref, bias_ref, alog_ref, col_ref, row_ref):
    v = _dot_nt(xn_ref[0], w_ref[...].astype(BF16)) + bias_ref[...]
    lane = lax.broadcasted_iota(jnp.int32, v.shape, 1)
    is_f = (lane >= GATE_F) & (lane < GATE_DT)
    is_dt = (lane >= GATE_DT) & (lane < GATE_CS)
    logf = -_softplus(-v)
    dt = _softplus(v)
    da = jnp.where(is_dt, dt * (-jnp.exp(alog_ref[...])), 0.0)
    pre = jnp.where(is_f, logf, 0.0) + pltpu.roll(da, GATE_CS - GATE_DT, axis=1)
    cum = _cumsum_rows(pre) * LOG2E
    out = jnp.where(lane < GATE_F, v * LOG2E, jnp.where(is_f, cum, jnp.where(is_dt, dt, cum)))
    col_ref[0] = out
    row_ref[0] = out.T


def gate_prep(xn, w_gate_t, bias_row, alog_row):
    b, s, k = xn.shape
    w = w_gate_t.shape[0]
    return pl.pallas_call(
        _gate_prep_kernel,
        grid=(b, s // CHUNK),
        in_specs=[pl.BlockSpec((1, CHUNK, k), lambda i, c: (i, c, 0)),
                  pl.BlockSpec((w, k), lambda i, c: (0, 0)),
                  pl.BlockSpec((1, w), lambda i, c: (0, 0)),
                  pl.BlockSpec((1, w), lambda i, c: (0, 0))],
        out_specs=[pl.BlockSpec((1, CHUNK, w), lambda i, c: (i, c, 0)),
                   pl.BlockSpec((1, w, CHUNK), lambda i, c: (i, 0, c))],
        out_shape=[jax.ShapeDtypeStruct((b, s, w), F32),
                   jax.ShapeDtypeStruct((b, w, s), F32)],
        compiler_params=_cparams(("parallel", "parallel")),
        name="gate_prep",
    )(xn, w_gate_t, bias_row, alog_row)


def _lane_column(tile, idx):
    lane = lax.broadcasted_iota(jnp.int32, tile.shape, 1)
    return jnp.sum(jnp.where(lane == idx, tile, 0.0), axis=1, keepdims=True)


def _mlstm_kernel(q_ref, k_ref, v_ref, o_ref, gcol_ref, grow_ref, gain_ref, out_ref):
    h = pl.program_id(1)
    s_len = q_ref.shape[1]
    L = CHUNK
    scale = MLSTM_DQK ** -0.5
    ri = lax.broadcasted_iota(jnp.int32, (L, L), 0)
    ci = lax.broadcasted_iota(jnp.int32, (L, L), 1)
    causal = ci <= ri
    gain = gain_ref[0]

    c_state = jnp.zeros((MLSTM_DQK, MLSTM_DV), F32)
    n_state = jnp.zeros((1, MLSTM_DQK), F32)
    m_state = jnp.zeros((1, 1), F32)

    for c in range(s_len // L):
        rows = pl.ds(c * L, L)
        q = q_ref[0, rows, :] * scale
        k = k_ref[0, rows, :]
        qb = q.astype(BF16)
        kb = k.astype(BF16)
        vb = v_ref[0, rows, :].astype(BF16)
        gc = gcol_ref[0, rows, :]
        i_col = _lane_column(gc, GATE_I + h)
        b_col = _lane_column(gc, GATE_F + h)
        i_row = grow_ref[0, pl.ds(GATE_I + h, 1), rows]
        b_row = grow_ref[0, pl.ds(GATE_F + h, 1), rows]

        log_d = jnp.where(causal, b_col - (b_row - i_row), -jnp.inf)
        log_inter = b_col + m_state
        m_row = jnp.maximum(log_inter, jnp.max(log_d, axis=1, keepdims=True))
        dmat = jnp.exp2(log_d - m_row)
        scores = _dot_nt(qb, kb) * dmat
        inter = jnp.exp2(log_inter - m_row)
        num = _dot(scores.astype(BF16), vb) + inter * _dot(qb, c_state.astype(BF16))
        nq = jnp.sum(scores, axis=1, keepdims=True) + inter * jnp.sum(q * n_state, axis=1, keepdims=True)
        denom = jnp.maximum(jnp.abs(nq), jnp.exp2(-m_row))
        hm = num / denom
        y = _rms(hm, gain) * _sigmoid(o_ref[0, rows, :])
        out_ref[0, rows, :] = y.astype(out_ref.dtype)

        btot = b_col[L - 1:L, :]
        lw = btot - b_col + i_col
        m_new = jnp.maximum(btot + m_state, jnp.max(lw, axis=0, keepdims=True))
        decay = jnp.exp2(btot + m_state - m_new)
        wk = jnp.exp2(lw - m_new) * k
        c_state = decay * c_state + _dot(wk.T.astype(BF16), vb)
        n_state = decay * n_state + jnp.sum(wk, axis=0, keepdims=True)
        m_state = m_new


def mlstm(proj, gcol, grow, gain):
    b, s, _ = proj.shape
    qk_blk = lambda off: pl.BlockSpec((1, s, MLSTM_DQK), lambda i, h: (i, 0, off // MLSTM_DQK + h))
    v_blk = lambda off: pl.BlockSpec((1, s, MLSTM_DV), lambda i, h: (i, 0, off // MLSTM_DV + h))
    return pl.pallas_call(
        _mlstm_kernel,
        grid=(b, MLSTM_HEADS),
        in_specs=[qk_blk(OFF_Q), qk_blk(OFF_K), v_blk(OFF_V), v_blk(OFF_O),
                  pl.BlockSpec((1, s, GATE_W), lambda i, h: (i, 0, 0)),
                  pl.BlockSpec((1, GATE_W, s), lambda i, h: (i, 0, 0)),
                  pl.BlockSpec((1, 1, MLSTM_DV), lambda i, h: (h, 0, 0))],
        out_specs=pl.BlockSpec((1, s, MLSTM_DV), lambda i, h: (i, 0, h)),
        out_shape=jax.ShapeDtypeStruct((b, s, MLSTM_HEADS * MLSTM_DV), BF16),
        compiler_params=_cparams(("parallel", "parallel")),
        name="mlstm",
    )(proj, proj, proj, proj, gcol, grow, gain.reshape(MLSTM_HEADS, 1, MLSTM_DV))


def _conv_silu(ref, c, w_ref, b_ref):
    L = CHUNK
    cur = ref[0, pl.ds(c * L, L), :]
    if c == 0:
        halo = jnp.zeros((CONV_HALO, cur.shape[1]), F32)
    else:
        halo = ref[0, pl.ds(c * L - CONV_HALO, CONV_HALO), :]
    ext = jnp.concatenate([halo, cur], axis=0)
    w = w_ref[...]
    acc = b_ref[...] + w[SSD_CONV - 1:SSD_CONV, :] * cur
    for back in range(1, SSD_CONV):
        shifted = pltpu.roll(ext, back, axis=0)[CONV_HALO:, :]
        acc = acc + w[SSD_CONV - 1 - back:SSD_CONV - back, :] * shifted
    return _silu(acc)


def _per_head_lanes(vals, shape):
    lane = lax.broadcasted_iota(jnp.int32, shape, 1)
    out = jnp.broadcast_to(vals[SSD_HEADS_PER_GROUP - 1], shape)
    for r in range(SSD_HEADS_PER_GROUP - 2, -1, -1):
        out = jnp.where(lane < (r + 1) * SSD_HEADDIM, vals[r], out)
    return out


def _ssd_kernel(x_ref, b_ref, c_ref, z_ref, wx_ref, wb_ref, wc_ref, bx_ref, bb_ref, bc_ref,
                gcol_ref, grow_ref, d_ref, gain_ref, out_ref):
    g = pl.program_id(1)
    s_len = x_ref.shape[1]
    L = CHUNK
    R = SSD_HEADS_PER_GROUP
    ri = lax.broadcasted_iota(jnp.int32, (L, L), 0)
    ci = lax.broadcasted_iota(jnp.int32, (L, L), 1)
    causal = ci <= ri
    lane_head = lax.broadcasted_iota(jnp.int32, (L, SSD_GROUP_W), 1) // SSD_HEADDIM

    state = jnp.zeros((SSD_STATE, SSD_GROUP_W), F32)

    for c in range(s_len // L):
        rows = pl.ds(c * L, L)
        xs = _conv_silu(x_ref, c, wx_ref, bx_ref)
        bm = _conv_silu(b_ref, c, wb_ref, bb_ref)
        cm = _conv_silu(c_ref, c, wc_ref, bc_ref)
        bmb = bm.astype(BF16)
        cmb = cm.astype(BF16)
        gc = gcol_ref[0, rows, :]
        dt_cols = [_lane_column(gc, GATE_DT + R * g + r) for r in range(R)]
        cs_cols = [_lane_column(gc, GATE_CS + R * g + r) for r in range(R)]
        cs_rows = [grow_ref[0, pl.ds(GATE_CS + R * g + r, 1), rows] for r in range(R)]

        dt_full = _per_head_lanes(dt_cols, (L, SSD_GROUP_W))
        cs_full = _per_head_lanes(cs_cols, (L, SSD_GROUP_W))
        cs_end = _per_head_lanes([col[L - 1:L, :] for col in cs_cols], (1, SSD_GROUP_W))

        xd = xs * dt_full
        xdb = xd.astype(BF16)
        cb = _dot_nt(cmb, bmb)
        y = _dot(cmb, state.astype(BF16)) * jnp.exp2(cs_full)
        for r in range(R):
            lmat = jnp.where(causal, jnp.exp2(cs_cols[r] - cs_rows[r]), 0.0)
            xr = jnp.where(lane_head == r, xdb, jnp.zeros_like(xdb))
            y = y + _dot((cb * lmat).astype(BF16), xr)
        y = y + d_ref[...] * xs
        y = y * _silu(z_ref[0, rows, :])
        out_ref[0, rows, :] = _rms(y, gain_ref[...]).astype(out_ref.dtype)

        decay_end = jnp.exp2(cs_end - cs_full)
        state = jnp.exp2(cs_end) * state + _dot(bm.T.astype(BF16), (xd * decay_end).astype(BF16))


def ssd(proj, conv_w, conv_b, gcol, grow, d_row, gain_row):
    b, s, _ = proj.shape
    G = SSD_GROUPS
    wide = lambda off: pl.BlockSpec((1, s, SSD_GROUP_W), lambda i, g: (i, 0, off // SSD_GROUP_W + g))
    narrow = lambda off: pl.BlockSpec((1, s, SSD_STATE), lambda i, g: (i, 0, off // SSD_STATE + g))
    cw = lambda rows, width, off: pl.BlockSpec((rows, width), lambda i, g: (0, off // width + g))
    return pl.pallas_call(
        _ssd_kernel,
        grid=(b, G),
        in_specs=[wide(OFF_X), narrow(OFF_B), narrow(OFF_C), wide(OFF_Z),
                  cw(SSD_CONV, SSD_GROUP_W, 0), cw(SSD_CONV, SSD_STATE, 2048), cw(SSD_CONV, SSD_STATE, 3072),
                  cw(1, SSD_GROUP_W, 0), cw(1, SSD_STATE, 2048), cw(1, SSD_STATE, 3072),
                  pl.BlockSpec((1, s, GATE_W), lambda i, g: (i, 0, 0)),
                  pl.BlockSpec((1, GATE_W, s), lambda i, g: (i, 0, 0)),
                  cw(1, SSD_GROUP_W, 0), cw(1, SSD_GROUP_W, 0)],
        out_specs=pl.BlockSpec((1, s, SSD_GROUP_W), lambda i, g: (i, 0, g)),
        out_shape=jax.ShapeDtypeStruct((b, s, SSD_D_INNER), BF16),
        compiler_params=_cparams(("parallel", "parallel")),
        name="ssd",
    )(proj, proj, proj, proj, conv_w, conv_w, conv_w, conv_b, conv_b, conv_b, gcol, grow, d_row, gain_row)


def _moba_kernel(q_ref, k_ref, v_ref, cos_ref, sin_ref, out_ref, qs_ref, qb_ref, kb_ref, vt_ref, s_ref, pb_ref):
    s_len = q_ref.shape[1]
    BLK = MOBA_BLOCK
    nb = s_len // BLK
    nbp = -(-nb // 8) * 8
    half = ATTN_HEAD_DIM // 2
    scale = ATTN_HEAD_DIM ** -0.5
    cos = cos_ref[...]
    sin = sin_ref[...]

    def rope(x):
        return x * cos + pltpu.roll(x, half, axis=1) * sin

    q = rope(q_ref[0])
    k = rope(k_ref[0])
    qs_ref[...] = q
    qb_ref[...] = (q * (scale * LOG2E)).astype(BF16)
    kb_ref[...] = k.astype(BF16)
    vt_ref[...] = v_ref[0].T.astype(BF16)

    row8 = lax.broadcasted_iota(jnp.int32, (nbp, ATTN_HEAD_DIM), 0)
    k_mean = jnp.zeros((nbp, ATTN_HEAD_DIM), F32)
    for j in range(nb):
        k_mean = jnp.where(row8 == j, jnp.mean(k[j * BLK:(j + 1) * BLK, :], axis=0, keepdims=True), k_mean)

    SUB = 8
    GRP = BLK // SUB
    gi = lax.broadcasted_iota(jnp.int32, (GRP, SUB, BLK), 0)
    si = lax.broadcasted_iota(jnp.int32, (GRP, SUB, BLK), 1)
    ci = lax.broadcasted_iota(jnp.int32, (GRP, SUB, BLK), 2)
    causal_bias = jnp.where(gi * SUB + si <= ci, 0.0, -jnp.inf)
    blk_row = lax.broadcasted_iota(jnp.int32, (nbp, BLK), 0)

    def scores(i):
        qrows = pl.ds(i * BLK, BLK)
        keep = None
        if i > MOBA_TOPK:
            gate = lax.dot_general(k_mean, qs_ref[qrows, :], (((1,), (1,)), ((), ())),
                                   precision=lax.Precision.HIGHEST, preferred_element_type=F32)
            rank = jnp.zeros((nbp, BLK), F32)
            for m in range(i):
                gm = gate[m:m + 1, :]
                ahead = (gm > gate) | ((gm == gate) & (blk_row > m))
                rank = rank + jnp.where(ahead, 1.0, 0.0)
            keep = rank < float(MOBA_TOPK)
        qb = qb_ref[qrows, :]
        m8 = None
        for j in range(i + 1):
            krows = pl.ds(j * BLK, BLK)
            sj = _dot_nt(kb_ref[krows, :], qb).reshape(GRP, SUB, BLK)
            if j == i:
                sj = sj + causal_bias
            elif keep is not None:
                sj = sj + jnp.broadcast_to(jnp.where(keep[j:j + 1, :], 0.0, -jnp.inf), (SUB, BLK))[None]
            s_ref[i % 2, krows, :] = sj.reshape(BLK, BLK)
            mj = jnp.max(sj, axis=0)
            m8 = mj if m8 is None else jnp.maximum(m8, mj)
        return m8

    def attend(i, m8):
        qrows = pl.ds(i * BLK, BLK)
        nk = (i + 1) * BLK
        m_row = jnp.broadcast_to(jnp.max(m8, axis=0, keepdims=True), (SUB, BLK))
        den8 = jnp.zeros((SUB, BLK), F32)
        for j in range(i + 1):
            krows = pl.ds(j * BLK, BLK)
            p = jnp.exp2(s_ref[i % 2, krows, :].reshape(GRP, SUB, BLK) - m_row[None])
            den8 = den8 + jnp.sum(p, axis=0)
            pb_ref[i % 2, krows, :] = p.reshape(BLK, BLK).astype(BF16)
        den = jnp.sum(den8, axis=0, keepdims=True)
        ot = _dot(vt_ref[:, pl.ds(0, nk)], pb_ref[i % 2, pl.ds(0, nk), :])
        out_ref[0, qrows, :] = (ot / den).T.astype(out_ref.dtype)

    m8_next = scores(0)
    for i in range(nb):
        m8_cur = m8_next
        if i + 1 < nb:
            m8_next = scores(i + 1)
        attend(i, m8_cur)


def moba(qkv, cos_full, sin_signed):
    b, s, _ = qkv.shape
    hd = ATTN_HEAD_DIM
    blk = lambda part: pl.BlockSpec((1, s, hd), lambda i, h: (i, 0, part * ATTN_HEADS + h))
    return pl.pallas_call(
        _moba_kernel,
        grid=(b, ATTN_HEADS),
        in_specs=[blk(0), blk(1), blk(2),
                  pl.BlockSpec((s, hd), lambda i, h: (0, 0)),
                  pl.BlockSpec((s, hd), lambda i, h: (0, 0))],
        out_specs=pl.BlockSpec((1, s, hd), lambda i, h: (i, 0, h)),
        out_shape=jax.ShapeDtypeStruct((b, s, ATTN_HEADS * hd), BF16),
        scratch_shapes=[pltpu.VMEM((s, hd), F32), pltpu.VMEM((s, hd), BF16), pltpu.VMEM((s, hd), BF16),
                        pltpu.VMEM((hd, s), BF16), pltpu.VMEM((2, s, MOBA_BLOCK), F32),
                        pltpu.VMEM((2, s, MOBA_BLOCK), BF16)],
        compiler_params=_cparams(("parallel", "parallel")),
        name="moba",
    )(qkv, qkv, qkv, cos_full, sin_signed)


def _rope_tables(s):
    half = ATTN_HEAD_DIM // 2
    inv = ROPE_THETA ** (-jnp.arange(half, dtype=F32) / half)
    ang = jnp.arange(s).astype(F32)[:, None] * inv[None, :]
    cos, sin = jnp.cos(ang), jnp.sin(ang)
    return jnp.concatenate([cos, cos], axis=-1), jnp.concatenate([-sin, sin], axis=-1)


def kernel(x, l0_norm_mix, l0_w_in, l0_mlstm_gate_bias, l0_mlstm_norm, l0_ssd_conv_w, l0_ssd_conv_b,
           l0_ssd_dt_bias, l0_ssd_a_log, l0_ssd_d, l0_ssd_norm, l0_w_out, l0_norm_ffn, l0_ffn_gate,
           l0_ffn_up, l0_ffn_down, l1_norm_mix, l1_w_qkv, l1_w_o, l1_norm_ffn, l1_ffn_gate, l1_ffn_up,
           l1_ffn_down, final_norm):
    b, s, d = x.shape
    m = b * s
    assert d == D_MODEL and s % CHUNK == 0 and s % MOBA_BLOCK == 0 and m % 2048 == 0
    xr = x.reshape(m, d)

    xn = rmsnorm(xr, l0_norm_mix, tm=512, out_dtype=BF16, name="l0_norm")
    w_in_t = l0_w_in.T
    proj_a = wres_matmul(xn, w_in_t, 0, IN_QKVO_END, w_transposed=True, tm=1024, tn=1024, out_dtype=F32,
                         name="in_proj_qkvo").reshape(b, s, -1)
    proj_b = wres_matmul(xn, w_in_t, IN_ZX_START, IN_ZX_END - IN_ZX_START, w_transposed=True, tm=1024, tn=1024,
                         out_dtype=F32, name="in_proj_zx").reshape(b, s, -1)
    w_gate_t = jnp.concatenate([w_in_t[IN_QKVO_END:IN_ZX_START], w_in_t[IN_ZX_END:IN_END],
                                jnp.zeros((GATE_W - GATE_COLS, d), F32)], axis=0)
    pad = jnp.zeros((GATE_W - GATE_COLS,), F32)
    bias_row = jnp.concatenate([l0_mlstm_gate_bias, l0_ssd_dt_bias, pad]).reshape(1, GATE_W)
    alog_row = jnp.concatenate([jnp.zeros((GATE_DT,), F32), l0_ssd_a_log, pad]).reshape(1, GATE_W)
    gcol, grow = gate_prep(xn.reshape(b, s, d), w_gate_t, bias_row, alog_row)

    hm = mlstm(proj_a, gcol, grow, l0_mlstm_norm).reshape(m, D_MODEL)
    d_row = jnp.repeat(l0_ssd_d, SSD_HEADDIM).reshape(1, SSD_D_INNER)
    ys = ssd(proj_b, l0_ssd_conv_w, l0_ssd_conv_b.reshape(1, -1), gcol, grow, d_row,
             l0_ssd_norm.reshape(1, SSD_D_INNER)).reshape(m, SSD_D_INNER)
    xr, xn = matmul_residual_norm([hm, ys], l0_w_out.astype(BF16), xr, l0_norm_ffn, tm=512, emit_x=True,
                                  norm_dtype=BF16, name="l0_out_proj")
    hff = swiglu_up(xn, l0_ffn_gate, l0_ffn_up, tm=2048, tn=512, name="l0_ffn_up")
    xr, xn = matmul_residual_norm([hff], l0_ffn_down.astype(BF16), xr, l1_norm_mix, tm=256, emit_x=True,
                                  norm_dtype=BF16, name="l0_ffn_down")

    qkv = wres_matmul(xn, l1_w_qkv, 0, 3 * D_MODEL, w_transposed=False, tm=1024, tn=1024, out_dtype=F32,
                      name="qkv_proj").reshape(b, s, 3 * D_MODEL)
    cos_full, sin_signed = _rope_tables(s)
    att = moba(qkv, cos_full, sin_signed).reshape(m, D_MODEL)
    xr, xn = matmul_residual_norm([att], l1_w_o.astype(BF16), xr, l1_norm_ffn, tm=512, emit_x=True,
                                  norm_dtype=BF16, name="l1_out_proj")
    hff = swiglu_up(xn, l1_ffn_gate, l1_ffn_up, tm=2048, tn=512, name="l1_ffn_up")
    (out,) = matmul_residual_norm([hff], l1_ffn_down.astype(BF16), xr, final_norm, tm=256, emit_x=False,
                                  norm_dtype=F32, name="l1_ffn_down")
    return out.reshape(b, s, d)
```

```python
import functools

import jax
import jax.numpy as jnp
from jax import lax
from jax.experimental import pallas as pl
from jax.experimental.pallas import tpu as pltpu

F32 = jnp.float32
BF16 = jnp.bfloat16

D_MODEL = 2048
NORM_EPS = 1e-6
MLSTM_HEADS = 8
MLSTM_DQK = 128
MLSTM_DV = 256
SSD_D_INNER = 2048
SSD_HEADDIM = 64
SSD_HEADS = 32
SSD_STATE = 128
SSD_GROUPS = 8
SSD_HEADS_PER_GROUP = SSD_HEADS // SSD_GROUPS
SSD_GROUP_W = SSD_HEADS_PER_GROUP * SSD_HEADDIM
SSD_CONV = 4
ATTN_HEADS = 16
ATTN_HEAD_DIM = 128
MOBA_BLOCK = 256
MOBA_TOPK = 3
ROPE_THETA = 10000.0
LOG2E = 1.4426950408889634

IN_QKVO_END = 6144
IN_ZX_START, IN_ZX_END = 6160, 12304
IN_END = 12336
OFF_Q, OFF_K, OFF_V, OFF_O = 0, 1024, 2048, 4096
OFF_Z, OFF_X, OFF_B, OFF_C = 0, 2048, 4096, 5120
GATE_I, GATE_F, GATE_DT, GATE_CS = 0, 8, 16, 48
GATE_COLS = 48
GATE_W = 128

CHUNK = 256
SSD_CHUNK = 128
CONV_HALO = 8
BF16_TILE_ROWS = 16
TAIL_ROWS = 128
WCAST_ROWS = 256

V7X_VMEM_LIMIT_BYTES = 56 * 1024 * 1024


def _cparams(sem):
    return pltpu.CompilerParams(dimension_semantics=sem, vmem_limit_bytes=V7X_VMEM_LIMIT_BYTES)


def _sigmoid(x):
    return 1.0 / (1.0 + jnp.exp(-x))


def _silu(x):
    return x * _sigmoid(x)


def _softplus(x):
    return jnp.maximum(x, 0.0) + jnp.log(1.0 + jnp.exp(-jnp.abs(x)))


def _rms(x, g):
    ms = jnp.mean(x * x, axis=-1, keepdims=True)
    return x * lax.rsqrt(ms + NORM_EPS) * g


def _dot(a, b):
    return jnp.dot(a, b, preferred_element_type=F32)


def _dot_nt(a, b):
    return lax.dot_general(a, b, (((1,), (1,)), ((), ())), preferred_element_type=F32)


def _side_cast_specs(w, grid):
    n_steps = grid[0] * grid[1]
    rows, cols = w.shape
    assert rows % n_steps == 0 and (rows // n_steps) % BF16_TILE_ROWS == 0
    blk = pl.BlockSpec((rows // n_steps, cols), lambda a, b_: (a * grid[1] + b_, 0))
    return blk, blk, jax.ShapeDtypeStruct((rows, cols), BF16)


def _swiglu_up_kernel(x_ref, wg_ref, wu_ref, side_ref, o_ref, side_out_ref, wg_scr, wu_scr):
    side_out_ref[...] = side_ref[...].astype(BF16)

    @pl.when(pl.program_id(1) == 0)
    def _():
        for r in range(0, wg_scr.shape[0], WCAST_ROWS):
            rows = pl.ds(r, WCAST_ROWS)
            wg_scr[rows, :] = wg_ref[rows, :].astype(BF16)
            wu_scr[rows, :] = wu_ref[rows, :].astype(BF16)

    x = x_ref[...]
    gate = _dot(x, wg_scr[...])
    up = _dot(x, wu_scr[...])
    o_ref[...] = (_silu(gate) * up).astype(o_ref.dtype)


def swiglu_up(xn, wg, wu, w_side, *, tm, tn, name):
    m, k = xn.shape
    n = wg.shape[1]
    grid = (n // tn, m // tm)
    side_in, side_out, side_shape = _side_cast_specs(w_side, grid)
    return pl.pallas_call(
        _swiglu_up_kernel,
        grid=grid,
        in_specs=[pl.BlockSpec((tm, k), lambda j, i: (i, 0)),
                  pl.BlockSpec((k, tn), lambda j, i: (0, j)),
                  pl.BlockSpec((k, tn), lambda j, i: (0, j)),
                  side_in],
        out_specs=[pl.BlockSpec((tm, tn), lambda j, i: (i, j)), side_out],
        out_shape=[jax.ShapeDtypeStruct((m, n), BF16), side_shape],
        scratch_shapes=[pltpu.VMEM((k, tn), BF16), pltpu.VMEM((k, tn), BF16)],
        compiler_params=_cparams(("parallel", "arbitrary")),
        name=name,
    )(xn, wg, wu, w_side)


def _matmul_residual_norm_kernel(*refs, n_a, emit_x):
    a_refs = refs[:n_a]
    w_ref, r_ref, g_ref = refs[n_a:n_a + 3]
    outs = refs[n_a + 3:]
    acc = r_ref[...]
    off = 0
    for a_ref in a_refs:
        kk = a_ref.shape[1]
        acc = acc + _dot(a_ref[...], w_ref[pl.ds(off, kk), :])
        off += kk
    if emit_x:
        outs[0][...] = acc
    xn_ref = outs[-1]
    xn_ref[...] = _rms(acc, g_ref[...]).astype(xn_ref.dtype)


def matmul_residual_norm(a_list, w, res, g, *, tm, emit_x, norm_dtype, name):
    m, n = res.shape
    k = w.shape[0]
    assert sum(a.shape[1] for a in a_list) == k
    row_blk = lambda width: pl.BlockSpec((tm, width), lambda i: (i, 0))
    out_specs = [row_blk(n)]
    out_shape = [jax.ShapeDtypeStruct((m, n), norm_dtype)]
    if emit_x:
        out_specs = [row_blk(n)] + out_specs
        out_shape = [jax.ShapeDtypeStruct((m, n), F32)] + out_shape
    return pl.pallas_call(
        functools.partial(_matmul_residual_norm_kernel, n_a=len(a_list), emit_x=emit_x),
        grid=(m // tm,),
        in_specs=[row_blk(a.shape[1]) for a in a_list] + [
            pl.BlockSpec((k, n), lambda i: (0, 0), pipeline_mode=pl.Buffered(1)),
            row_blk(n),
            pl.BlockSpec((1, n), lambda i: (0, 0))],
        out_specs=out_specs,
        out_shape=out_shape,
        compiler_params=_cparams(("parallel",)),
        name=name,
    )(*a_list, w, res, g.reshape(1, n))


def _wres_matmul_kernel(x_ref, *refs, shift, w_transposed):
    if shift == 0:
        wa_ref, o_ref, w_scr = refs
    else:
        wa_ref, wb_ref, o_ref, w_scr = refs

    @pl.when(pl.program_id(1) == 0)
    def _():
        if shift == 0:
            w_scr[...] = wa_ref[...].astype(BF16)
        else:
            tn = w_scr.shape[0]
            w_scr[pl.ds(0, tn - shift), :] = wa_ref[pl.ds(shift, tn - shift), :].astype(BF16)
            w_scr[pl.ds(tn - shift, shift), :] = wb_ref[pl.ds(0, shift), :].astype(BF16)

    mm = _dot_nt if w_transposed else _dot
    o_ref[...] = mm(x_ref[...], w_scr[...]).astype(o_ref.dtype)


def wres_matmul(x, w, start, n_out, *, w_transposed, tm, tn, out_dtype, name):
    m, k = x.shape
    shift = start % tn
    base = start - shift
    assert n_out % tn == 0 and (shift == 0 or (w_transposed and shift % BF16_TILE_ROWS == 0 and shift <= TAIL_ROWS))
    assert start + n_out <= w.shape[0 if w_transposed else 1]
    if w_transposed:
        w_specs = [pl.BlockSpec((tn, k), lambda j, i: (base // tn + j, 0))]
        if shift:
            w_specs.append(pl.BlockSpec((TAIL_ROWS, k), lambda j, i: ((base + (j + 1) * tn) // TAIL_ROWS, 0)))
        scratch = pltpu.VMEM((tn, k), BF16)
    else:
        w_specs = [pl.BlockSpec((k, tn), lambda j, i: (0, base // tn + j))]
        scratch = pltpu.VMEM((k, tn), BF16)
    return pl.pallas_call(
        functools.partial(_wres_matmul_kernel, shift=shift, w_transposed=w_transposed),
        grid=(n_out // tn, m // tm),
        in_specs=[pl.BlockSpec((tm, k), lambda j, i: (i, 0))] + w_specs,
        out_specs=pl.BlockSpec((tm, tn), lambda j, i: (i, j)),
        out_shape=jax.ShapeDtypeStruct((m, n_out), out_dtype),
        scratch_shapes=[scratch],
        compiler_params=_cparams(("parallel", "arbitrary")),
        name=name,
    )(x, *([w] * len(w_specs)))


def _cumsum_rows(x):
    n = x.shape[0]
    row = lax.broadcasted_iota(jnp.int32, x.shape, 0)
    shift = 1
    while shift < n:
        x = x + jnp.where(row >= shift, pltpu.roll(x, shift, axis=0), 0.0)
        shift *= 2
    return x


def _norm_gate_prep_kernel(x_ref, g_ref, w_ref, bias_ref, alog_ref, xn_ref, col_ref, row_ref):
    xn = _rms(x_ref[0], g_ref[...]).astype(BF16)
    xn_ref[0] = xn
    v = _dot_nt(xn, w_ref[...].astype(BF16)) + bias_ref[...]
    lane = lax.broadcasted_iota(jnp.int32, v.shape, 1)
    is_f = (lane >= GATE_F) & (lane < GATE_DT)
    is_dt = (lane >= GATE_DT) & (lane < GATE_CS)
    logf = -_softplus(-v)
    dt = _softplus(v)
    da = jnp.where(is_dt, dt * (-jnp.exp(alog_ref[...])), 0.0)
    pre = jnp.where(is_f, logf, 0.0) + pltpu.roll(da, GATE_CS - GATE_DT, axis=1)
    cum = _cumsum_rows(pre) * LOG2E
    out = jnp.where(lane < GATE_F, v * LOG2E, jnp.where(is_f, cum, jnp.where(is_dt, dt, cum)))
    col_ref[0] = out
    row_ref[0] = out.T


def norm_gate_prep(x, g, w_gate_t, bias_row, alog_row):
    b, s, k = x.shape
    w = w_gate_t.shape[0]
    return pl.pallas_call(
        _norm_gate_prep_kernel,
        grid=(b, s // CHUNK),
        in_specs=[pl.BlockSpec((1, CHUNK, k), lambda i, c: (i, c, 0)),
                  pl.BlockSpec((1, k), lambda i, c: (0, 0)),
                  pl.BlockSpec((w, k), lambda i, c: (0, 0)),
                  pl.BlockSpec((1, w), lambda i, c: (0, 0)),
                  pl.BlockSpec((1, w), lambda i, c: (0, 0))],
        out_specs=[pl.BlockSpec((1, CHUNK, k), lambda i, c: (i, c, 0)),
                   pl.BlockSpec((1, CHUNK, w), lambda i, c: (i, c, 0)),
                   pl.BlockSpec((1, w, CHUNK), lambda i, c: (i, 0, c))],
        out_shape=[jax.ShapeDtypeStruct((b, s, k), BF16),
                   jax.ShapeDtypeStruct((b, s, w), F32),
                   jax.ShapeDtypeStruct((b, w, s), F32)],
        compiler_params=_cparams(("parallel", "parallel")),
        name="norm_gate_prep",
    )(x, g.reshape(1, k), w_gate_t, bias_row, alog_row)


def _lane_column(tile, idx):
    lane = lax.broadcasted_iota(jnp.int32, tile.shape, 1)
    return jnp.sum(jnp.where(lane == idx, tile, 0.0), axis=1, keepdims=True)


def _mlstm_kernel(q_ref, k_ref, v_ref, o_ref, gcol_ref, grow_ref, gain_ref, side_ref, out_ref, side_out_ref):
    side_out_ref[...] = side_ref[...].astype(BF16)
    h = pl.program_id(1)
    s_len = q_ref.shape[1]
    L = CHUNK
    scale = MLSTM_DQK ** -0.5
    ri = lax.broadcasted_iota(jnp.int32, (L, L), 0)
    ci = lax.broadcasted_iota(jnp.int32, (L, L), 1)
    causal = ci <= ri
    gain = gain_ref[0]

    c_state = jnp.zeros((MLSTM_DQK, MLSTM_DV), F32)
    n_state = jnp.zeros((1, MLSTM_DQK), F32)
    m_state = jnp.zeros((1, 1), F32)

    for c in range(s_len // L):
        rows = pl.ds(c * L, L)
        q = q_ref[0, rows, :] * scale
        k = k_ref[0, rows, :]
        qb = q.astype(BF16)
        kb = k.astype(BF16)
        vb = v_ref[0, rows, :].astype(BF16)
        gc = gcol_ref[0, rows, :]
        i_col = _lane_column(gc, GATE_I + h)
        b_col = _lane_column(gc, GATE_F + h)
        i_row = grow_ref[0, pl.ds(GATE_I + h, 1), rows]
        b_row = grow_ref[0, pl.ds(GATE_F + h, 1), rows]

        log_d = jnp.where(causal, b_col - (b_row - i_row), -jnp.inf)
        log_inter = b_col + m_state
        m_row = jnp.maximum(log_inter, jnp.max(log_d, axis=1, keepdims=True))
        dmat = jnp.exp2(log_d - m_row)
        scores = _dot_nt(qb, kb) * dmat
        inter = jnp.exp2(log_inter - m_row)
        num = _dot(scores.astype(BF16), vb) + inter * _dot(qb, c_state.astype(BF16))
        nq = jnp.sum(scores, axis=1, keepdims=True) + inter * jnp.sum(q * n_state, axis=1, keepdims=True)
        denom = jnp.maximum(jnp.abs(nq), jnp.exp2(-m_row))
        hm = num / denom
        y = _rms(hm, gain) * _sigmoid(o_ref[0, rows, :])
        out_ref[0, rows, :] = y.astype(out_ref.dtype)

        btot = b_col[L - 1:L, :]
        lw = btot - b_col + i_col
        m_new = jnp.maximum(btot + m_state, jnp.max(lw, axis=0, keepdims=True))
        decay = jnp.exp2(btot + m_state - m_new)
        wk = jnp.exp2(lw - m_new) * k
        c_state = decay * c_state + _dot(wk.T.astype(BF16), vb)
        n_state = decay * n_state + jnp.sum(wk, axis=0, keepdims=True)
        m_state = m_new


def mlstm(proj, gcol, grow, gain, w_side):
    b, s, _ = proj.shape
    grid = (b, MLSTM_HEADS)
    side_in, side_out, side_shape = _side_cast_specs(w_side, grid)
    qk_blk = lambda off: pl.BlockSpec((1, s, MLSTM_DQK), lambda i, h: (i, 0, off // MLSTM_DQK + h))
    v_blk = lambda off: pl.BlockSpec((1, s, MLSTM_DV), lambda i, h: (i, 0, off // MLSTM_DV + h))
    return pl.pallas_call(
        _mlstm_kernel,
        grid=grid,
        in_specs=[qk_blk(OFF_Q), qk_blk(OFF_K), v_blk(OFF_V), v_blk(OFF_O),
                  pl.BlockSpec((1, s, GATE_W), lambda i, h: (i, 0, 0)),
                  pl.BlockSpec((1, GATE_W, s), lambda i, h: (i, 0, 0)),
                  pl.BlockSpec((1, 1, MLSTM_DV), lambda i, h: (h, 0, 0)),
                  side_in],
        out_specs=[pl.BlockSpec((1, s, MLSTM_DV), lambda i, h: (i, 0, h)), side_out],
        out_shape=[jax.ShapeDtypeStruct((b, s, MLSTM_HEADS * MLSTM_DV), BF16), side_shape],
        compiler_params=_cparams(("parallel", "parallel")),
        name="mlstm",
    )(proj, proj, proj, proj, gcol, grow, gain.reshape(MLSTM_HEADS, 1, MLSTM_DV), w_side)


def _conv_silu(ref, c, w_ref, b_ref):
    L = SSD_CHUNK
    cur = ref[0, pl.ds(c * L, L), :]
    if c == 0:
        halo = jnp.zeros((CONV_HALO, cur.shape[1]), F32)
    else:
        halo = ref[0, pl.ds(c * L - CONV_HALO, CONV_HALO), :]
    ext = jnp.concatenate([halo, cur], axis=0)
    w = w_ref[...]
    acc = b_ref[...] + w[SSD_CONV - 1:SSD_CONV, :] * cur
    for back in range(1, SSD_CONV):
        shifted = pltpu.roll(ext, back, axis=0)[CONV_HALO:, :]
        acc = acc + w[SSD_CONV - 1 - back:SSD_CONV - back, :] * shifted
    return _silu(acc)


def _per_head_lanes(vals, shape):
    lane = lax.broadcasted_iota(jnp.int32, shape, 1)
    out = jnp.broadcast_to(vals[SSD_HEADS_PER_GROUP - 1], shape)
    for r in range(SSD_HEADS_PER_GROUP - 2, -1, -1):
        out = jnp.where(lane < (r + 1) * SSD_HEADDIM, vals[r], out)
    return out


def _ssd_kernel(x_ref, b_ref, c_ref, z_ref, wx_ref, wb_ref, wc_ref, bx_ref, bb_ref, bc_ref,
                gcol_ref, grow_ref, d_ref, gain_ref, out_ref):
    g = pl.program_id(1)
    s_len = x_ref.shape[1]
    L = SSD_CHUNK
    R = SSD_HEADS_PER_GROUP
    ri = lax.broadcasted_iota(jnp.int32, (L, L), 0)
    ci = lax.broadcasted_iota(jnp.int32, (L, L), 1)
    causal = ci <= ri
    lane_head = lax.broadcasted_iota(jnp.int32, (L, SSD_GROUP_W), 1) // SSD_HEADDIM

    state = jnp.zeros((SSD_STATE, SSD_GROUP_W), F32)
    cs_prev_end = None

    for c in range(s_len // L):
        rows = pl.ds(c * L, L)
        xs = _conv_silu(x_ref, c, wx_ref, bx_ref)
        bm = _conv_silu(b_ref, c, wb_ref, bb_ref)
        cm = _conv_silu(c_ref, c, wc_ref, bc_ref)
        bmb = bm.astype(BF16)
        cmb = cm.astype(BF16)
        gc = gcol_ref[0, rows, :]
        dt_cols = [_lane_column(gc, GATE_DT + R * g + r) for r in range(R)]
        cs_cols = [_lane_column(gc, GATE_CS + R * g + r) for r in range(R)]
        base, half = (c * L) // CHUNK * CHUNK, (c * L) % CHUNK
        cs_rows = [grow_ref[0, pl.ds(GATE_CS + R * g + r, 1), pl.ds(base, CHUNK)][:, half:half + L]
                   for r in range(R)]
        cs_raw_end = [col[L - 1:L, :] for col in cs_cols]
        if half:
            cs_cols = [col - off for col, off in zip(cs_cols, cs_prev_end)]
            cs_rows = [row - off for row, off in zip(cs_rows, cs_prev_end)]
        cs_prev_end = cs_raw_end

        dt_full = _per_head_lanes(dt_cols, (L, SSD_GROUP_W))
        cs_full = _per_head_lanes(cs_cols, (L, SSD_GROUP_W))
        cs_end = _per_head_lanes([col[L - 1:L, :] for col in cs_cols], (1, SSD_GROUP_W))

        xd = xs * dt_full
        xdb = xd.astype(BF16)
        cb = _dot_nt(cmb, bmb)
        y = _dot(cmb, state.astype(BF16)) * jnp.exp2(cs_full)
        for r in range(R):
            lmat = jnp.where(causal, jnp.exp2(cs_cols[r] - cs_rows[r]), 0.0)
            xr = jnp.where(lane_head == r, xdb, jnp.zeros_like(xdb))
            y = y + _dot((cb * lmat).astype(BF16), xr)
        y = y + d_ref[...] * xs
        y = y * _silu(z_ref[0, rows, :])
        out_ref[0, rows, :] = _rms(y, gain_ref[...]).astype(out_ref.dtype)

        decay_end = jnp.exp2(cs_end - cs_full)
        state = jnp.exp2(cs_end) * state + _dot(bm.T.astype(BF16), (xd * decay_end).astype(BF16))


def ssd(proj, conv_w, conv_b, gcol, grow, d_row, gain_row):
    b, s, _ = proj.shape
    G = SSD_GROUPS
    wide = lambda off: pl.BlockSpec((1, s, SSD_GROUP_W), lambda i, g: (i, 0, off // SSD_GROUP_W + g))
    narrow = lambda off: pl.BlockSpec((1, s, SSD_STATE), lambda i, g: (i, 0, off // SSD_STATE + g))
    cw = lambda rows, width, off: pl.BlockSpec((rows, width), lambda i, g: (0, off // width + g))
    return pl.pallas_call(
        _ssd_kernel,
        grid=(b, G),
        in_specs=[wide(OFF_X), narrow(OFF_B), narrow(OFF_C), wide(OFF_Z),
                  cw(SSD_CONV, SSD_GROUP_W, 0), cw(SSD_CONV, SSD_STATE, 2048), cw(SSD_CONV, SSD_STATE, 3072),
                  cw(1, SSD_GROUP_W, 0), cw(1, SSD_STATE, 2048), cw(1, SSD_STATE, 3072),
                  pl.BlockSpec((1, s, GATE_W), lambda i, g: (i, 0, 0)),
                  pl.BlockSpec((1, GATE_W, s), lambda i, g: (i, 0, 0)),
                  cw(1, SSD_GROUP_W, 0), cw(1, SSD_GROUP_W, 0)],
        out_specs=pl.BlockSpec((1, s, SSD_GROUP_W), lambda i, g: (i, 0, g)),
        out_shape=jax.ShapeDtypeStruct((b, s, SSD_D_INNER), BF16),
        compiler_params=_cparams(("parallel", "parallel")),
        name="ssd",
    )(proj, proj, proj, proj, conv_w, conv_w, conv_w, conv_b, conv_b, conv_b, gcol, grow, d_row, gain_row)


def _moba_kernel(q_ref, k_ref, v_ref, cos_ref, sin_ref, side_ref, out_ref, side_out_ref,
                 qs_ref, qb_ref, kb_ref, vt_ref, s_ref, pb_ref):
    side_out_ref[...] = side_ref[...].astype(BF16)
    s_len = q_ref.shape[1]
    BLK = MOBA_BLOCK
    nb = s_len // BLK
    nbp = -(-nb // 8) * 8
    half = ATTN_HEAD_DIM // 2
    scale = ATTN_HEAD_DIM ** -0.5
    cos = cos_ref[...]
    sin = sin_ref[...]

    def rope(x):
        return x * cos + pltpu.roll(x, half, axis=1) * sin

    q = rope(q_ref[0])
    k = rope(k_ref[0])
    qs_ref[...] = q
    qb_ref[...] = (q * (scale * LOG2E)).astype(BF16)
    kb_ref[...] = k.astype(BF16)
    vt_ref[...] = v_ref[0].T.astype(BF16)

    row8 = lax.broadcasted_iota(jnp.int32, (nbp, ATTN_HEAD_DIM), 0)
    k_mean = jnp.zeros((nbp, ATTN_HEAD_DIM), F32)
    for j in range(nb):
        k_mean = jnp.where(row8 == j, jnp.mean(k[j * BLK:(j + 1) * BLK, :], axis=0, keepdims=True), k_mean)

    SUB = 8
    GRP = BLK // SUB
    gi = lax.broadcasted_iota(jnp.int32, (GRP, SUB, BLK), 0)
    si = lax.broadcasted_iota(jnp.int32, (GRP, SUB, BLK), 1)
    ci = lax.broadcasted_iota(jnp.int32, (GRP, SUB, BLK), 2)
    causal_bias = jnp.where(gi * SUB + si <= ci, 0.0, -jnp.inf)
    blk_row = lax.broadcasted_iota(jnp.int32, (nbp, BLK), 0)

    def scores(i):
        qrows = pl.ds(i * BLK, BLK)
        keep = None
        if i > MOBA_TOPK:
            gate = lax.dot_general(k_mean, qs_ref[qrows, :], (((1,), (1,)), ((), ())),
                                   precision=lax.Precision.HIGHEST, preferred_element_type=F32)
            rank = jnp.zeros((nbp, BLK), F32)
            for m in range(i):
                gm = gate[m:m + 1, :]
                ahead = (gm > gate) | ((gm == gate) & (blk_row > m))
                rank = rank + jnp.where(ahead, 1.0, 0.0)
            keep = rank < float(MOBA_TOPK)
        qb = qb_ref[qrows, :]
        m8 = None
        for j in range(i + 1):
            krows = pl.ds(j * BLK, BLK)
            sj = _dot_nt(kb_ref[krows, :], qb).reshape(GRP, SUB, BLK)
            if j == i:
                sj = sj + causal_bias
            elif keep is not None:
                sj = sj + jnp.broadcast_to(jnp.where(keep[j:j + 1, :], 0.0, -jnp.inf), (SUB, BLK))[None]
            s_ref[i % 2, krows, :] = sj.reshape(BLK, BLK)
            mj = jnp.max(sj, axis=0)
            m8 = mj if m8 is None else jnp.maximum(m8, mj)
        return m8

    def attend(i, m8):
        qrows = pl.ds(i * BLK, BLK)
        nk = (i + 1) * BLK
        m_row = jnp.broadcast_to(jnp.max(m8, axis=0, keepdims=True), (SUB, BLK))
        den8 = jnp.zeros((SUB, BLK), F32)
        for j in range(i + 1):
            krows = pl.ds(j * BLK, BLK)
            p = jnp.exp2(s_ref[i % 2, krows, :].reshape(GRP, SUB, BLK) - m_row[None])
            den8 = den8 + jnp.sum(p, axis=0)
            pb_ref[i % 2, krows, :] = p.reshape(BLK, BLK).astype(BF16)
        den = jnp.sum(den8, axis=0, keepdims=True)
        ot = _dot(vt_ref[:, pl.ds(0, nk)], pb_ref[i % 2, pl.ds(0, nk), :])
        out_ref[0, qrows, :] = (ot / den).T.astype(out_ref.dtype)

    m8_next = scores(0)
    for i in range(nb):
        m8_cur = m8_next
        if i + 1 < nb:
            m8_next = scores(i + 1)
        attend(i, m8_cur)


def moba(qkv, cos_full, sin_signed, w_side):
    b, s, _ = qkv.shape
    hd = ATTN_HEAD_DIM
    grid = (b, ATTN_HEADS)
    side_in, side_out, side_shape = _side_cast_specs(w_side, grid)
    blk = lambda part: pl.BlockSpec((1, s, hd), lambda i, h: (i, 0, part * ATTN_HEADS + h))
    return pl.pallas_call(
        _moba_kernel,
        grid=grid,
        in_specs=[blk(0), blk(1), blk(2),
                  pl.BlockSpec((s, hd), lambda i, h: (0, 0)),
                  pl.BlockSpec((s, hd), lambda i, h: (0, 0)),
                  side_in],
        out_specs=[pl.BlockSpec((1, s, hd), lambda i, h: (i, 0, h)), side_out],
        out_shape=[jax.ShapeDtypeStruct((b, s, ATTN_HEADS * hd), BF16), side_shape],
        scratch_shapes=[pltpu.VMEM((s, hd), F32), pltpu.VMEM((s, hd), BF16), pltpu.VMEM((s, hd), BF16),
                        pltpu.VMEM((hd, s), BF16), pltpu.VMEM((2, s, MOBA_BLOCK), F32),
                        pltpu.VMEM((2, s, MOBA_BLOCK), BF16)],
        compiler_params=_cparams(("parallel", "parallel")),
        name="moba",
    )(qkv, qkv, qkv, cos_full, sin_signed, w_side)


def _rope_tables(s):
    half = ATTN_HEAD_DIM // 2
    inv = ROPE_THETA ** (-jnp.arange(half, dtype=F32) / half)
    ang = jnp.arange(s).astype(F32)[:, None] * inv[None, :]
    cos, sin = jnp.cos(ang), jnp.sin(ang)
    return jnp.concatenate([cos, cos], axis=-1), jnp.concatenate([-sin, sin], axis=-1)


def kernel(x, l0_norm_mix, l0_w_in, l0_mlstm_gate_bias, l0_mlstm_norm, l0_ssd_conv_w, l0_ssd_conv_b,
           l0_ssd_dt_bias, l0_ssd_a_log, l0_ssd_d, l0_ssd_norm, l0_w_out, l0_norm_ffn, l0_ffn_gate,
           l0_ffn_up, l0_ffn_down, l1_norm_mix, l1_w_qkv, l1_w_o, l1_norm_ffn, l1_ffn_gate, l1_ffn_up,
           l1_ffn_down, final_norm):
    b, s, d = x.shape
    m = b * s
    assert d == D_MODEL and s % CHUNK == 0 and s % MOBA_BLOCK == 0 and m % 1024 == 0
    xr = x.reshape(m, d)

    w_in_t = l0_w_in.T
    w_gate_t = jnp.concatenate([w_in_t[IN_QKVO_END:IN_ZX_START], w_in_t[IN_ZX_END:IN_END],
                                jnp.zeros((GATE_W - GATE_COLS, d), F32)], axis=0)
    pad = jnp.zeros((GATE_W - GATE_COLS,), F32)
    bias_row = jnp.concatenate([l0_mlstm_gate_bias, l0_ssd_dt_bias, pad]).reshape(1, GATE_W)
    alog_row = jnp.concatenate([jnp.zeros((GATE_DT,), F32), l0_ssd_a_log, pad]).reshape(1, GATE_W)
    xn, gcol, grow = norm_gate_prep(x, l0_norm_mix, w_gate_t, bias_row, alog_row)
    xn = xn.reshape(m, d)
    proj_a = wres_matmul(xn, w_in_t, 0, IN_QKVO_END, w_transposed=True, tm=1024, tn=1024, out_dtype=F32,
                         name="in_proj_qkvo").reshape(b, s, -1)
    proj_b = wres_matmul(xn, w_in_t, IN_ZX_START, IN_ZX_END - IN_ZX_START, w_transposed=True, tm=1024, tn=1024,
                         out_dtype=F32, name="in_proj_zx").reshape(b, s, -1)

    hm, w_out_bf16 = mlstm(proj_a, gcol, grow, l0_mlstm_norm, l0_w_out)
    hm = hm.reshape(m, D_MODEL)
    d_row = jnp.repeat(l0_ssd_d, SSD_HEADDIM).reshape(1, SSD_D_INNER)
    ys = ssd(proj_b, l0_ssd_conv_w, l0_ssd_conv_b.reshape(1, -1), gcol, grow, d_row,
             l0_ssd_norm.reshape(1, SSD_D_INNER)).reshape(m, SSD_D_INNER)
    xr, xn = matmul_residual_norm([hm, ys], w_out_bf16, xr, l0_norm_ffn, tm=512, emit_x=True,
                                  norm_dtype=BF16, name="l0_out_proj")
    hff, w_down_bf16 = swiglu_up(xn, l0_ffn_gate, l0_ffn_up, l0_ffn_down, tm=1024, tn=512, name="l0_ffn_up")
    xr, xn = matmul_residual_norm([hff], w_down_bf16, xr, l1_norm_mix, tm=256, emit_x=True,
                                  norm_dtype=BF16, name="l0_ffn_down")

    qkv = wres_matmul(xn, l1_w_qkv, 0, 3 * D_MODEL, w_transposed=False, tm=1024, tn=1024, out_dtype=F32,
                      name="qkv_proj").reshape(b, s, 3 * D_MODEL)
    cos_full, sin_signed = _rope_tables(s)
    att, w_o_bf16 = moba(qkv, cos_full, sin_signed, l1_w_o)
    xr, xn = matmul_residual_norm([att.reshape(m, D_MODEL)], w_o_bf16, xr, l1_norm_ffn, tm=512, emit_x=True,
                                  norm_dtype=BF16, name="l1_out_proj")
    hff, w_down_bf16 = swiglu_up(xn, l1_ffn_gate, l1_ffn_up, l1_ffn_down, tm=1024, tn=512, name="l1_ffn_up")
    (out,) = matmul_residual_norm([hff], w_down_bf16, xr, final_norm, tm=256, emit_x=False,
                                  norm_dtype=F32, name="l1_ffn_down")
    return out.reshape(b, s, d)
```

```python
import functools

import jax
import jax.numpy as jnp
from jax import lax
from jax.experimental import pallas as pl
from jax.experimental.pallas import tpu as pltpu

F32 = jnp.float32
BF16 = jnp.bfloat16

D_MODEL = 2048
NORM_EPS = 1e-6
MLSTM_HEADS = 8
MLSTM_DQK = 128
MLSTM_DV = 256
SSD_D_INNER = 2048
SSD_HEADDIM = 64
SSD_HEADS = 32
SSD_STATE = 128
SSD_GROUPS = 8
SSD_HEADS_PER_GROUP = SSD_HEADS // SSD_GROUPS
SSD_GROUP_W = SSD_HEADS_PER_GROUP * SSD_HEADDIM
SSD_CONV = 4
ATTN_HEADS = 16
ATTN_HEAD_DIM = 128
MOBA_BLOCK = 256
MOBA_TOPK = 3
ROPE_THETA = 10000.0
LOG2E = 1.4426950408889634

IN_QKVO_END = 6144
IN_ZX_START, IN_ZX_END = 6160, 12304
IN_END = 12336
OFF_Q, OFF_K, OFF_V, OFF_O = 0, 1024, 2048, 4096
OFF_Z, OFF_X, OFF_B, OFF_C = 0, 2048, 4096, 5120
GATE_I, GATE_F, GATE_DT, GATE_CS = 0, 8, 16, 48
GATE_COLS = 48
GATE_W = 128

CHUNK = 256
SSD_CHUNK = 128
NORM_GATE_ROWS = 1024
CONV_HALO = 8
BF16_TILE_ROWS = 16
TAIL_ROWS = 128
WCAST_ROWS = 256

V7X_VMEM_LIMIT_BYTES = 56 * 1024 * 1024


def _cparams(sem):
    return pltpu.CompilerParams(dimension_semantics=sem, vmem_limit_bytes=V7X_VMEM_LIMIT_BYTES)


def _sigmoid(x):
    return 1.0 / (1.0 + jnp.exp(-x))


def _silu(x):
    return x * _sigmoid(x)


def _softplus(x):
    return jnp.maximum(x, 0.0) + jnp.log(1.0 + jnp.exp(-jnp.abs(x)))


def _rms(x, g):
    ms = jnp.mean(x * x, axis=-1, keepdims=True)
    return x * lax.rsqrt(ms + NORM_EPS) * g


def _dot(a, b):
    return jnp.dot(a, b, preferred_element_type=F32)


def _dot_nt(a, b):
    return lax.dot_general(a, b, (((1,), (1,)), ((), ())), preferred_element_type=F32)


def _side_cast_specs(w, grid):
    n_steps = grid[0] * grid[1]
    rows, cols = w.shape
    assert rows % n_steps == 0 and (rows // n_steps) % BF16_TILE_ROWS == 0
    blk = pl.BlockSpec((rows // n_steps, cols), lambda a, b_: (a * grid[1] + b_, 0))
    return blk, blk, jax.ShapeDtypeStruct((rows, cols), BF16)


def _swiglu_up_kernel(x_ref, wg_ref, wu_ref, side_ref, o_ref, side_out_ref, wg_scr, wu_scr):
    side_out_ref[...] = side_ref[...].astype(BF16)

    @pl.when(pl.program_id(1) == 0)
    def _():
        for r in range(0, wg_scr.shape[0], WCAST_ROWS):
            rows = pl.ds(r, WCAST_ROWS)
            wg_scr[rows, :] = wg_ref[rows, :].astype(BF16)
            wu_scr[rows, :] = wu_ref[rows, :].astype(BF16)

    x = x_ref[...]
    gate = _dot(x, wg_scr[...])
    up = _dot(x, wu_scr[...])
    o_ref[...] = (_silu(gate) * up).astype(o_ref.dtype)


def swiglu_up(xn, wg, wu, w_side, *, tm, tn, name):
    m, k = xn.shape
    n = wg.shape[1]
    grid = (n // tn, m // tm)
    side_in, side_out, side_shape = _side_cast_specs(w_side, grid)
    return pl.pallas_call(
        _swiglu_up_kernel,
        grid=grid,
        in_specs=[pl.BlockSpec((tm, k), lambda j, i: (i, 0)),
                  pl.BlockSpec((k, tn), lambda j, i: (0, j)),
                  pl.BlockSpec((k, tn), lambda j, i: (0, j)),
                  side_in],
        out_specs=[pl.BlockSpec((tm, tn), lambda j, i: (i, j)), side_out],
        out_shape=[jax.ShapeDtypeStruct((m, n), BF16), side_shape],
        scratch_shapes=[pltpu.VMEM((k, tn), BF16), pltpu.VMEM((k, tn), BF16)],
        compiler_params=_cparams(("parallel", "arbitrary")),
        name=name,
    )(xn, wg, wu, w_side)


def _matmul_residual_norm_kernel(*refs, n_a, emit_x):
    a_refs = refs[:n_a]
    w_ref, r_ref, g_ref = refs[n_a:n_a + 3]
    outs = refs[n_a + 3:]
    acc = r_ref[...]
    off = 0
    for a_ref in a_refs:
        kk = a_ref.shape[1]
        acc = acc + _dot(a_ref[...], w_ref[pl.ds(off, kk), :])
        off += kk
    if emit_x:
        outs[0][...] = acc
    xn_ref = outs[-1]
    xn_ref[...] = _rms(acc, g_ref[...]).astype(xn_ref.dtype)


def matmul_residual_norm(a_list, w, res, g, *, tm, emit_x, norm_dtype, name):
    m, n = res.shape
    k = w.shape[0]
    assert sum(a.shape[1] for a in a_list) == k
    row_blk = lambda width: pl.BlockSpec((tm, width), lambda i: (i, 0))
    out_specs = [row_blk(n)]
    out_shape = [jax.ShapeDtypeStruct((m, n), norm_dtype)]
    if emit_x:
        out_specs = [row_blk(n)] + out_specs
        out_shape = [jax.ShapeDtypeStruct((m, n), F32)] + out_shape
    return pl.pallas_call(
        functools.partial(_matmul_residual_norm_kernel, n_a=len(a_list), emit_x=emit_x),
        grid=(m // tm,),
        in_specs=[row_blk(a.shape[1]) for a in a_list] + [
            pl.BlockSpec((k, n), lambda i: (0, 0), pipeline_mode=pl.Buffered(1)),
            row_blk(n),
            pl.BlockSpec((1, n), lambda i: (0, 0))],
        out_specs=out_specs,
        out_shape=out_shape,
        compiler_params=_cparams(("parallel",)),
        name=name,
    )(*a_list, w, res, g.reshape(1, n))


def _wres_matmul_kernel(x_ref, *refs, shift, w_transposed):
    if shift == 0:
        wa_ref, o_ref, w_scr = refs
    else:
        wa_ref, wb_ref, o_ref, w_scr = refs

    @pl.when(pl.program_id(1) == 0)
    def _():
        if shift == 0:
            w_scr[...] = wa_ref[...].astype(BF16)
        else:
            tn = w_scr.shape[0]
            w_scr[pl.ds(0, tn - shift), :] = wa_ref[pl.ds(shift, tn - shift), :].astype(BF16)
            w_scr[pl.ds(tn - shift, shift), :] = wb_ref[pl.ds(0, shift), :].astype(BF16)

    mm = _dot_nt if w_transposed else _dot
    o_ref[...] = mm(x_ref[...], w_scr[...]).astype(o_ref.dtype)


def wres_matmul(x, w, start, n_out, *, w_transposed, tm, tn, out_dtype, name):
    m, k = x.shape
    shift = start % tn
    base = start - shift
    assert n_out % tn == 0 and (shift == 0 or (w_transposed and shift % BF16_TILE_ROWS == 0 and shift <= TAIL_ROWS))
    assert start + n_out <= w.shape[0 if w_transposed else 1]
    if w_transposed:
        w_specs = [pl.BlockSpec((tn, k), lambda j, i: (base // tn + j, 0))]
        if shift:
            w_specs.append(pl.BlockSpec((TAIL_ROWS, k), lambda j, i: ((base + (j + 1) * tn) // TAIL_ROWS, 0)))
        scratch = pltpu.VMEM((tn, k), BF16)
    else:
        w_specs = [pl.BlockSpec((k, tn), lambda j, i: (0, base // tn + j))]
        scratch = pltpu.VMEM((k, tn), BF16)
    return pl.pallas_call(
        functools.partial(_wres_matmul_kernel, shift=shift, w_transposed=w_transposed),
        grid=(n_out // tn, m // tm),
        in_specs=[pl.BlockSpec((tm, k), lambda j, i: (i, 0))] + w_specs,
        out_specs=pl.BlockSpec((tm, tn), lambda j, i: (i, j)),
        out_shape=jax.ShapeDtypeStruct((m, n_out), out_dtype),
        scratch_shapes=[scratch],
        compiler_params=_cparams(("parallel", "arbitrary")),
        name=name,
    )(x, *([w] * len(w_specs)))


def _cumsum_rows(x):
    n = x.shape[0]
    row = lax.broadcasted_iota(jnp.int32, x.shape, 0)
    shift = 1
    while shift < n:
        x = x + jnp.where(row >= shift, pltpu.roll(x, shift, axis=0), 0.0)
        shift *= 2
    return x


def _norm_gate_prep_kernel(x_ref, g_ref, w_ref, bias_ref, alog_ref, xn_ref, col_ref, row_ref):
    w = w_ref[...].astype(BF16)
    neg_a = -jnp.exp(alog_ref[...])
    for c in range(x_ref.shape[1] // CHUNK):
        rows = pl.ds(c * CHUNK, CHUNK)
        xn = _rms(x_ref[0, rows, :], g_ref[...]).astype(BF16)
        xn_ref[0, rows, :] = xn
        v = _dot_nt(xn, w) + bias_ref[...]
        lane = lax.broadcasted_iota(jnp.int32, v.shape, 1)
        is_f = (lane >= GATE_F) & (lane < GATE_DT)
        is_dt = (lane >= GATE_DT) & (lane < GATE_CS)
        logf = -_softplus(-v)
        dt = _softplus(v)
        da = jnp.where(is_dt, dt * neg_a, 0.0)
        pre = jnp.where(is_f, logf, 0.0) + pltpu.roll(da, GATE_CS - GATE_DT, axis=1)
        cum = _cumsum_rows(pre) * LOG2E
        out = jnp.where(lane < GATE_F, v * LOG2E, jnp.where(is_f, cum, jnp.where(is_dt, dt, cum)))
        col_ref[0, rows, :] = out
        row_ref[0, :, rows] = out.T


def norm_gate_prep(x, g, w_gate_t, bias_row, alog_row):
    b, s, k = x.shape
    w = w_gate_t.shape[0]
    tr = NORM_GATE_ROWS
    return pl.pallas_call(
        _norm_gate_prep_kernel,
        grid=(b, s // tr),
        in_specs=[pl.BlockSpec((1, tr, k), lambda i, c: (i, c, 0)),
                  pl.BlockSpec((1, k), lambda i, c: (0, 0)),
                  pl.BlockSpec((w, k), lambda i, c: (0, 0)),
                  pl.BlockSpec((1, w), lambda i, c: (0, 0)),
                  pl.BlockSpec((1, w), lambda i, c: (0, 0))],
        out_specs=[pl.BlockSpec((1, tr, k), lambda i, c: (i, c, 0)),
                   pl.BlockSpec((1, tr, w), lambda i, c: (i, c, 0)),
                   pl.BlockSpec((1, w, tr), lambda i, c: (i, 0, c))],
        out_shape=[jax.ShapeDtypeStruct((b, s, k), BF16),
                   jax.ShapeDtypeStruct((b, s, w), F32),
                   jax.ShapeDtypeStruct((b, w, s), F32)],
        compiler_params=_cparams(("parallel", "parallel")),
        name="norm_gate_prep",
    )(x, g.reshape(1, k), w_gate_t, bias_row, alog_row)


def _lane_column(tile, idx):
    lane = lax.broadcasted_iota(jnp.int32, tile.shape, 1)
    return jnp.sum(jnp.where(lane == idx, tile, 0.0), axis=1, keepdims=True)


def _mlstm_kernel(q_ref, k_ref, v_ref, o_ref, gcol_ref, grow_ref, gain_ref, side_ref, out_ref, side_out_ref):
    side_out_ref[...] = side_ref[...].astype(BF16)
    h = pl.program_id(1)
    s_len = q_ref.shape[1]
    L = CHUNK
    scale = MLSTM_DQK ** -0.5
    ri = lax.broadcasted_iota(jnp.int32, (L, L), 0)
    ci = lax.broadcasted_iota(jnp.int32, (L, L), 1)
    causal = ci <= ri
    gain = gain_ref[0]

    def intra(c):
        rows = pl.ds(c * L, L)
        q = q_ref[0, rows, :] * scale
        k = k_ref[0, rows, :]
        qb = q.astype(BF16)
        kb = k.astype(BF16)
        vb = v_ref[0, rows, :].astype(BF16)
        gc = gcol_ref[0, rows, :]
        i_col = _lane_column(gc, GATE_I + h)
        b_col = _lane_column(gc, GATE_F + h)
        i_row = grow_ref[0, pl.ds(GATE_I + h, 1), rows]
        b_row = grow_ref[0, pl.ds(GATE_F + h, 1), rows]
        log_d = jnp.where(causal, b_col - (b_row - i_row), -jnp.inf)
        btot = b_col[L - 1:L, :]
        lw = btot - b_col + i_col
        return dict(q=q, k=k, qb=qb, vb=vb, b_col=b_col, log_d=log_d, qk=_dot_nt(qb, kb), btot=btot, lw=lw,
                    m_intra=jnp.max(log_d, axis=1, keepdims=True), lw_max=jnp.max(lw, axis=0, keepdims=True))

    def finish(c, t, carry):
        c_state, n_state, m_state = carry
        rows = pl.ds(c * L, L)
        log_inter = t["b_col"] + m_state
        m_row = jnp.maximum(log_inter, t["m_intra"])
        scores = t["qk"] * jnp.exp2(t["log_d"] - m_row)
        inter = jnp.exp2(log_inter - m_row)
        num = _dot(scores.astype(BF16), t["vb"]) + inter * _dot(t["qb"], c_state.astype(BF16))
        nq = jnp.sum(scores, axis=1, keepdims=True) + inter * jnp.sum(t["q"] * n_state, axis=1, keepdims=True)
        denom = jnp.maximum(jnp.abs(nq), jnp.exp2(-m_row))
        hm = num / denom
        y = _rms(hm, gain) * _sigmoid(o_ref[0, rows, :])
        out_ref[0, rows, :] = y.astype(out_ref.dtype)
        m_new = jnp.maximum(t["btot"] + m_state, t["lw_max"])
        decay = jnp.exp2(t["btot"] + m_state - m_new)
        wk = jnp.exp2(t["lw"] - m_new) * t["k"]
        c_state = decay * c_state + _dot(wk.T.astype(BF16), t["vb"])
        n_state = decay * n_state + jnp.sum(wk, axis=0, keepdims=True)
        return c_state, n_state, m_new

    carry = (jnp.zeros((MLSTM_DQK, MLSTM_DV), F32), jnp.zeros((1, MLSTM_DQK), F32), jnp.zeros((1, 1), F32))
    n_chunks = s_len // L
    nxt = intra(0)
    for c in range(n_chunks):
        cur = nxt
        if c + 1 < n_chunks:
            nxt = intra(c + 1)
        carry = finish(c, cur, carry)


def mlstm(proj, gcol, grow, gain, w_side):
    b, s, _ = proj.shape
    grid = (b, MLSTM_HEADS)
    side_in, side_out, side_shape = _side_cast_specs(w_side, grid)
    qk_blk = lambda off: pl.BlockSpec((1, s, MLSTM_DQK), lambda i, h: (i, 0, off // MLSTM_DQK + h))
    v_blk = lambda off: pl.BlockSpec((1, s, MLSTM_DV), lambda i, h: (i, 0, off // MLSTM_DV + h))
    return pl.pallas_call(
        _mlstm_kernel,
        grid=grid,
        in_specs=[qk_blk(OFF_Q), qk_blk(OFF_K), v_blk(OFF_V), v_blk(OFF_O),
                  pl.BlockSpec((1, s, GATE_W), lambda i, h: (i, 0, 0)),
                  pl.BlockSpec((1, GATE_W, s), lambda i, h: (i, 0, 0)),
                  pl.BlockSpec((1, 1, MLSTM_DV), lambda i, h: (h, 0, 0)),
                  side_in],
        out_specs=[pl.BlockSpec((1, s, MLSTM_DV), lambda i, h: (i, 0, h)), side_out],
        out_shape=[jax.ShapeDtypeStruct((b, s, MLSTM_HEADS * MLSTM_DV), BF16), side_shape],
        compiler_params=_cparams(("parallel", "parallel")),
        name="mlstm",
    )(proj, proj, proj, proj, gcol, grow, gain.reshape(MLSTM_HEADS, 1, MLSTM_DV), w_side)


def _conv_silu(ref, c, w_ref, b_ref):
    L = SSD_CHUNK
    cur = ref[0, pl.ds(c * L, L), :]
    if c == 0:
        halo = jnp.zeros((CONV_HALO, cur.shape[1]), F32)
    else:
        halo = ref[0, pl.ds(c * L - CONV_HALO, CONV_HALO), :]
    ext = jnp.concatenate([halo, cur], axis=0)
    w = w_ref[...]
    acc = b_ref[...] + w[SSD_CONV - 1:SSD_CONV, :] * cur
    for back in range(1, SSD_CONV):
        shifted = pltpu.roll(ext, back, axis=0)[CONV_HALO:, :]
        acc = acc + w[SSD_CONV - 1 - back:SSD_CONV - back, :] * shifted
    return _silu(acc)


def _per_head_lanes(vals, shape):
    lane = lax.broadcasted_iota(jnp.int32, shape, 1)
    out = jnp.broadcast_to(vals[SSD_HEADS_PER_GROUP - 1], shape)
    for r in range(SSD_HEADS_PER_GROUP - 2, -1, -1):
        out = jnp.where(lane < (r + 1) * SSD_HEADDIM, vals[r], out)
    return out


def _ssd_kernel(x_ref, b_ref, c_ref, z_ref, wx_ref, wb_ref, wc_ref, bx_ref, bb_ref, bc_ref,
                gcol_ref, grow_ref, d_ref, gain_ref, out_ref):
    g = pl.program_id(1)
    s_len = x_ref.shape[1]
    L = SSD_CHUNK
    R = SSD_HEADS_PER_GROUP
    ri = lax.broadcasted_iota(jnp.int32, (L, L), 0)
    ci = lax.broadcasted_iota(jnp.int32, (L, L), 1)
    causal = ci <= ri
    lane_head = lax.broadcasted_iota(jnp.int32, (L, SSD_GROUP_W), 1) // SSD_HEADDIM

    state = jnp.zeros((SSD_STATE, SSD_GROUP_W), F32)
    cs_prev_end = None

    for c in range(s_len // L):
        rows = pl.ds(c * L, L)
        xs = _conv_silu(x_ref, c, wx_ref, bx_ref)
        bm = _conv_silu(b_ref, c, wb_ref, bb_ref)
        cm = _conv_silu(c_ref, c, wc_ref, bc_ref)
        bmb = bm.astype(BF16)
        cmb = cm.astype(BF16)
        gc = gcol_ref[0, rows, :]
        dt_cols = [_lane_column(gc, GATE_DT + R * g + r) for r in range(R)]
        cs_cols = [_lane_column(gc, GATE_CS + R * g + r) for r in range(R)]
        base, half = (c * L) // CHUNK * CHUNK, (c * L) % CHUNK
        cs_rows = [grow_ref[0, pl.ds(GATE_CS + R * g + r, 1), pl.ds(base, CHUNK)][:, half:half + L]
                   for r in range(R)]
        cs_raw_end = [col[L - 1:L, :] for col in cs_cols]
        if half:
            cs_cols = [col - off for col, off in zip(cs_cols, cs_prev_end)]
            cs_rows = [row - off for row, off in zip(cs_rows, cs_prev_end)]
        cs_prev_end = cs_raw_end

        dt_full = _per_head_lanes(dt_cols, (L, SSD_GROUP_W))
        cs_full = _per_head_lanes(cs_cols, (L, SSD_GROUP_W))
        cs_end = _per_head_lanes([col[L - 1:L, :] for col in cs_cols], (1, SSD_GROUP_W))

        xd = xs * dt_full
        xdb = xd.astype(BF16)
        cb = _dot_nt(cmb, bmb)
        y = _dot(cmb, state.astype(BF16)) * jnp.exp2(cs_full)
        for r in range(R):
            lmat = jnp.where(causal, jnp.exp2(cs_cols[r] - cs_rows[r]), 0.0)
            xr = jnp.where(lane_head == r, xdb, jnp.zeros_like(xdb))
            y = y + _dot((cb * lmat).astype(BF16), xr)
        y = y + d_ref[...] * xs
        y = y * _silu(z_ref[0, rows, :])
        out_ref[0, rows, :] = _rms(y, gain_ref[...]).astype(out_ref.dtype)

        decay_end = jnp.exp2(cs_end - cs_full)
        state = jnp.exp2(cs_end) * state + _dot(bm.T.astype(BF16), (xd * decay_end).astype(BF16))


def ssd(proj, conv_w, conv_b, gcol, grow, d_row, gain_row):
    b, s, _ = proj.shape
    G = SSD_GROUPS
    wide = lambda off: pl.BlockSpec((1, s, SSD_GROUP_W), lambda i, g: (i, 0, off // SSD_GROUP_W + g))
    narrow = lambda off: pl.BlockSpec((1, s, SSD_STATE), lambda i, g: (i, 0, off // SSD_STATE + g))
    cw = lambda rows, width, off: pl.BlockSpec((rows, width), lambda i, g: (0, off // width + g))
    return pl.pallas_call(
        _ssd_kernel,
        grid=(b, G),
        in_specs=[wide(OFF_X), narrow(OFF_B), narrow(OFF_C), wide(OFF_Z),
                  cw(SSD_CONV, SSD_GROUP_W, 0), cw(SSD_CONV, SSD_STATE, 2048), cw(SSD_CONV, SSD_STATE, 3072),
                  cw(1, SSD_GROUP_W, 0), cw(1, SSD_STATE, 2048), cw(1, SSD_STATE, 3072),
                  pl.BlockSpec((1, s, GATE_W), lambda i, g: (i, 0, 0)),
                  pl.BlockSpec((1, GATE_W, s), lambda i, g: (i, 0, 0)),
                  cw(1, SSD_GROUP_W, 0), cw(1, SSD_GROUP_W, 0)],
        out_specs=pl.BlockSpec((1, s, SSD_GROUP_W), lambda i, g: (i, 0, g)),
        out_shape=jax.ShapeDtypeStruct((b, s, SSD_D_INNER), BF16),
        compiler_params=_cparams(("parallel", "parallel")),
        name="ssd",
    )(proj, proj, proj, proj, conv_w, conv_w, conv_w, conv_b, conv_b, conv_b, gcol, grow, d_row, gain_row)


def _moba_kernel(q_ref, k_ref, v_ref, cos_ref, sin_ref, side_ref, out_ref, side_out_ref,
                 qs_ref, qb_ref, kb_ref, vt_ref, s_ref, pb_ref):
    side_out_ref[...] = side_ref[...].astype(BF16)
    s_len = q_ref.shape[1]
    BLK = MOBA_BLOCK
    nb = s_len // BLK
    nbp = -(-nb // 8) * 8
    half = ATTN_HEAD_DIM // 2
    scale = ATTN_HEAD_DIM ** -0.5
    cos = cos_ref[...]
    sin = sin_ref[...]

    def rope(x):
        return x * cos + pltpu.roll(x, half, axis=1) * sin

    q = rope(q_ref[0])
    k = rope(k_ref[0])
    qs_ref[...] = q
    qb_ref[...] = (q * (scale * LOG2E)).astype(BF16)
    kb_ref[...] = k.astype(BF16)
    vt_ref[...] = v_ref[0].T.astype(BF16)

    row8 = lax.broadcasted_iota(jnp.int32, (nbp, ATTN_HEAD_DIM), 0)
    k_mean = jnp.zeros((nbp, ATTN_HEAD_DIM), F32)
    for j in range(nb):
        k_mean = jnp.where(row8 == j, jnp.mean(k[j * BLK:(j + 1) * BLK, :], axis=0, keepdims=True), k_mean)

    SUB = 8
    GRP = BLK // SUB
    gi = lax.broadcasted_iota(jnp.int32, (GRP, SUB, BLK), 0)
    si = lax.broadcasted_iota(jnp.int32, (GRP, SUB, BLK), 1)
    ci = lax.broadcasted_iota(jnp.int32, (GRP, SUB, BLK), 2)
    causal_bias = jnp.where(gi * SUB + si <= ci, 0.0, -jnp.inf)
    blk_row = lax.broadcasted_iota(jnp.int32, (nbp, BLK), 0)

    def scores(i):
        qrows = pl.ds(i * BLK, BLK)
        keep = None
        if i > MOBA_TOPK:
            gate = lax.dot_general(k_mean, qs_ref[qrows, :], (((1,), (1,)), ((), ())),
                                   precision=lax.Precision.HIGHEST, preferred_element_type=F32)
            rank = jnp.zeros((nbp, BLK), F32)
            for m in range(i):
                gm = gate[m:m + 1, :]
                ahead = (gm > gate) | ((gm == gate) & (blk_row > m))
                rank = rank + jnp.where(ahead, 1.0, 0.0)
            keep = rank < float(MOBA_TOPK)
        qb = qb_ref[qrows, :]
        m8 = None
        for j in range(i + 1):
            krows = pl.ds(j * BLK, BLK)
            sj = _dot_nt(kb_ref[krows, :], qb).reshape(GRP, SUB, BLK)
            if j == i:
                sj = sj + causal_bias
            elif keep is not None:
                sj = sj + jnp.broadcast_to(jnp.where(keep[j:j + 1, :], 0.0, -jnp.inf), (SUB, BLK))[None]
            s_ref[i % 2, krows, :] = sj.reshape(BLK, BLK)
            mj = jnp.max(sj, axis=0)
            m8 = mj if m8 is None else jnp.maximum(m8, mj)
        return m8

    def attend(i, m8):
        qrows = pl.ds(i * BLK, BLK)
        nk = (i + 1) * BLK
        m_row = jnp.broadcast_to(jnp.max(m8, axis=0, keepdims=True), (SUB, BLK))
        den8 = jnp.zeros((SUB, BLK), F32)
        for j in range(i + 1):
            krows = pl.ds(j * BLK, BLK)
            p = jnp.exp2(s_ref[i % 2, krows, :].reshape(GRP, SUB, BLK) - m_row[None])
            den8 = den8 + jnp.sum(p, axis=0)
            pb_ref[i % 2, krows, :] = p.reshape(BLK, BLK).astype(BF16)
        den = jnp.sum(den8, axis=0, keepdims=True)
        ot = _dot(vt_ref[:, pl.ds(0, nk)], pb_ref[i % 2, pl.ds(0, nk), :])
        out_ref[0, qrows, :] = (ot / den).T.astype(out_ref.dtype)

    m8_next = scores(0)
    for i in range(nb):
        m8_cur = m8_next
        if i + 1 < nb:
            m8_next = scores(i + 1)
        attend(i, m8_cur)


def moba(qkv, cos_full, sin_signed, w_side):
    b, s, _ = qkv.shape
    hd = ATTN_HEAD_DIM
    grid = (b, ATTN_HEADS)
    side_in, side_out, side_shape = _side_cast_specs(w_side, grid)
    blk = lambda part: pl.BlockSpec((1, s, hd), lambda i, h: (i, 0, part * ATTN_HEADS + h))
    return pl.pallas_call(
        _moba_kernel,
        grid=grid,
        in_specs=[blk(0), blk(1), blk(2),
                  pl.BlockSpec((s, hd), lambda i, h: (0, 0)),
                  pl.BlockSpec((s, hd), lambda i, h: (0, 0)),
                  side_in],
        out_specs=[pl.BlockSpec((1, s, hd), lambda i, h: (i, 0, h)), side_out],
        out_shape=[jax.ShapeDtypeStruct((b, s, ATTN_HEADS * hd), BF16), side_shape],
        scratch_shapes=[pltpu.VMEM((s, hd), F32), pltpu.VMEM((s, hd), BF16), pltpu.VMEM((s, hd), BF16),
                        pltpu.VMEM((hd, s), BF16), pltpu.VMEM((2, s, MOBA_BLOCK), F32),
                        pltpu.VMEM((2, s, MOBA_BLOCK), BF16)],
        compiler_params=_cparams(("parallel", "parallel")),
        name="moba",
    )(qkv, qkv, qkv, cos_full, sin_signed, w_side)


def _rope_tables(s):
    half = ATTN_HEAD_DIM // 2
    inv = ROPE_THETA ** (-jnp.arange(half, dtype=F32) / half)
    ang = jnp.arange(s).astype(F32)[:, None] * inv[None, :]
    cos, sin = jnp.cos(ang), jnp.sin(ang)
    return jnp.concatenate([cos, cos], axis=-1), jnp.concatenate([-sin, sin], axis=-1)


def kernel(x, l0_norm_mix, l0_w_in, l0_mlstm_gate_bias, l0_mlstm_norm, l0_ssd_conv_w, l0_ssd_conv_b,
           l0_ssd_dt_bias, l0_ssd_a_log, l0_ssd_d, l0_ssd_norm, l0_w_out, l0_norm_ffn, l0_ffn_gate,
           l0_ffn_up, l0_ffn_down, l1_norm_mix, l1_w_qkv, l1_w_o, l1_norm_ffn, l1_ffn_gate, l1_ffn_up,
           l1_ffn_down, final_norm):
    b, s, d = x.shape
    m = b * s
    assert d == D_MODEL and s % CHUNK == 0 and s % MOBA_BLOCK == 0 and m % 1024 == 0
    xr = x.reshape(m, d)

    w_in_t = l0_w_in.T
    w_gate_t = jnp.concatenate([w_in_t[IN_QKVO_END:IN_ZX_START], w_in_t[IN_ZX_END:IN_END],
                                jnp.zeros((GATE_W - GATE_COLS, d), F32)], axis=0)
    pad = jnp.zeros((GATE_W - GATE_COLS,), F32)
    bias_row = jnp.concatenate([l0_mlstm_gate_bias, l0_ssd_dt_bias, pad]).reshape(1, GATE_W)
    alog_row = jnp.concatenate([jnp.zeros((GATE_DT,), F32), l0_ssd_a_log, pad]).reshape(1, GATE_W)
    xn, gcol, grow = norm_gate_prep(x, l0_norm_mix, w_gate_t, bias_row, alog_row)
    xn = xn.reshape(m, d)
    proj_a = wres_matmul(xn, w_in_t, 0, IN_QKVO_END, w_transposed=True, tm=1024, tn=1024, out_dtype=F32,
                         name="in_proj_qkvo").reshape(b, s, -1)
    proj_b = wres_matmul(xn, w_in_t, IN_ZX_START, IN_ZX_END - IN_ZX_START, w_transposed=True, tm=1024, tn=1024,
                         out_dtype=F32, name="in_proj_zx").reshape(b, s, -1)

    hm, w_out_bf16 = mlstm(proj_a, gcol, grow, l0_mlstm_norm, l0_w_out)
    hm = hm.reshape(m, D_MODEL)
    d_row = jnp.repeat(l0_ssd_d, SSD_HEADDIM).reshape(1, SSD_D_INNER)
    ys = ssd(proj_b, l0_ssd_conv_w, l0_ssd_conv_b.reshape(1, -1), gcol, grow, d_row,
             l0_ssd_norm.reshape(1, SSD_D_INNER)).reshape(m, SSD_D_INNER)
    xr, xn = matmul_residual_norm([hm, ys], w_out_bf16, xr, l0_norm_ffn, tm=512, emit_x=True,
                                  norm_dtype=BF16, name="l0_out_proj")
    hff, w_down_bf16 = swiglu_up(xn, l0_ffn_gate, l0_ffn_up, l0_ffn_down, tm=1024, tn=512, name="l0_ffn_up")
    xr, xn = matmul_residual_norm([hff], w_down_bf16, xr, l1_norm_mix, tm=256, emit_x=True,
                                  norm_dtype=BF16, name="l0_ffn_down")

    qkv = wres_matmul(xn, l1_w_qkv, 0, 3 * D_MODEL, w_transposed=False, tm=1024, tn=1024, out_dtype=F32,
                      name="qkv_proj").reshape(b, s, 3 * D_MODEL)
    cos_full, sin_signed = _rope_tables(s)
    att, w_o_bf16 = moba(qkv, cos_full, sin_signed, l1_w_o)
    xr, xn = matmul_residual_norm([att.reshape(m, D_MODEL)], w_o_bf16, xr, l1_norm_ffn, tm=512, emit_x=True,
                                  norm_dtype=BF16, name="l1_out_proj")
    hff, w_down_bf16 = swiglu_up(xn, l1_ffn_gate, l1_ffn_up, l1_ffn_down, tm=1024, tn=512, name="l1_ffn_up")
    (out,) = matmul_residual_norm([hff], w_down_bf16, xr, final_norm, tm=256, emit_x=False,
                                  norm_dtype=F32, name="l1_ffn_down")
    return out.reshape(b, s, d)
```

```python
import functools

import jax
import jax.numpy as jnp
from jax import lax
from jax.experimental import pallas as pl
from jax.experimental.pallas import tpu as pltpu

F32 = jnp.float32
BF16 = jnp.bfloat16

D_MODEL = 2048
NORM_EPS = 1e-6
MLSTM_HEADS = 8
MLSTM_DQK = 128
MLSTM_DV = 256
SSD_D_INNER = 2048
SSD_HEADDIM = 64
SSD_HEADS = 32
SSD_STATE = 128
SSD_GROUPS = 8
SSD_HEADS_PER_GROUP = SSD_HEADS // SSD_GROUPS
SSD_GROUP_W = SSD_HEADS_PER_GROUP * SSD_HEADDIM
SSD_CONV = 4
ATTN_HEADS = 16
ATTN_HEAD_DIM = 128
MOBA_BLOCK = 256
MOBA_TOPK = 3
ROPE_THETA = 10000.0
LOG2E = 1.4426950408889634

IN_QKVO_END = 6144
IN_ZX_START, IN_ZX_END = 6160, 12304
IN_END = 12336
OFF_Q, OFF_K, OFF_V, OFF_O = 0, 1024, 2048, 4096
OFF_Z, OFF_X, OFF_B, OFF_C = 0, 2048, 4096, 5120
GATE_I, GATE_F, GATE_DT, GATE_CS = 0, 8, 16, 48
GATE_COLS = 48
GATE_W = 128

CHUNK = 256
SSD_CHUNK = 128
NORM_GATE_ROWS = 1024
CONV_HALO = 8
BF16_TILE_ROWS = 16
TAIL_ROWS = 128
WCAST_ROWS = 256

V7X_VMEM_LIMIT_BYTES = 56 * 1024 * 1024


def _cparams(sem):
    return pltpu.CompilerParams(dimension_semantics=sem, vmem_limit_bytes=V7X_VMEM_LIMIT_BYTES)


def _sigmoid(x):
    return 0.5 * jnp.tanh(0.5 * x) + 0.5


def _silu(x):
    h = 0.5 * x
    return h * jnp.tanh(h) + h


def _softplus(x):
    return jnp.maximum(x, 0.0) + jnp.log(1.0 + jnp.exp(-jnp.abs(x)))


def _rms(x, g):
    ms = jnp.mean(x * x, axis=-1, keepdims=True)
    return x * lax.rsqrt(ms + NORM_EPS) * g


def _dot(a, b):
    return jnp.dot(a, b, preferred_element_type=F32)


def _dot_nt(a, b):
    return lax.dot_general(a, b, (((1,), (1,)), ((), ())), preferred_element_type=F32)


def _side_cast_specs(w, grid):
    n_steps = grid[0] * grid[1]
    rows, cols = w.shape
    assert rows % n_steps == 0 and (rows // n_steps) % BF16_TILE_ROWS == 0
    blk = pl.BlockSpec((rows // n_steps, cols), lambda a, b_: (a * grid[1] + b_, 0))
    return blk, blk, jax.ShapeDtypeStruct((rows, cols), BF16)


def _swiglu_up_kernel(x_ref, wg_ref, wu_ref, side_ref, o_ref, side_out_ref, wg_scr, wu_scr):
    side_out_ref[...] = side_ref[...].astype(BF16)

    @pl.when(pl.program_id(1) == 0)
    def _():
        for r in range(0, wg_scr.shape[0], WCAST_ROWS):
            rows = pl.ds(r, WCAST_ROWS)
            wg_scr[rows, :] = wg_ref[rows, :].astype(BF16)
            wu_scr[rows, :] = wu_ref[rows, :].astype(BF16)

    x = x_ref[...]
    gate = _dot(x, wg_scr[...])
    up = _dot(x, wu_scr[...])
    o_ref[...] = (_silu(gate) * up).astype(o_ref.dtype)


def swiglu_up(xn, wg, wu, w_side, *, tm, tn, name):
    m, k = xn.shape
    n = wg.shape[1]
    grid = (n // tn, m // tm)
    side_in, side_out, side_shape = _side_cast_specs(w_side, grid)
    return pl.pallas_call(
        _swiglu_up_kernel,
        grid=grid,
        in_specs=[pl.BlockSpec((tm, k), lambda j, i: (i, 0)),
                  pl.BlockSpec((k, tn), lambda j, i: (0, j)),
                  pl.BlockSpec((k, tn), lambda j, i: (0, j)),
                  side_in],
        out_specs=[pl.BlockSpec((tm, tn), lambda j, i: (i, j)), side_out],
        out_shape=[jax.ShapeDtypeStruct((m, n), BF16), side_shape],
        scratch_shapes=[pltpu.VMEM((k, tn), BF16), pltpu.VMEM((k, tn), BF16)],
        compiler_params=_cparams(("parallel", "arbitrary")),
        name=name,
    )(xn, wg, wu, w_side)


def _matmul_residual_norm_kernel(*refs, n_a, emit_x):
    a_refs = refs[:n_a]
    w_ref, r_ref, g_ref = refs[n_a:n_a + 3]
    outs = refs[n_a + 3:]
    acc = r_ref[...]
    off = 0
    for a_ref in a_refs:
        kk = a_ref.shape[1]
        acc = acc + _dot(a_ref[...], w_ref[pl.ds(off, kk), :])
        off += kk
    if emit_x:
        outs[0][...] = acc
    xn_ref = outs[-1]
    xn_ref[...] = _rms(acc, g_ref[...]).astype(xn_ref.dtype)


def matmul_residual_norm(a_list, w, res, g, *, tm, emit_x, norm_dtype, name):
    m, n = res.shape
    k = w.shape[0]
    assert sum(a.shape[1] for a in a_list) == k
    row_blk = lambda width: pl.BlockSpec((tm, width), lambda i: (i, 0))
    out_specs = [row_blk(n)]
    out_shape = [jax.ShapeDtypeStruct((m, n), norm_dtype)]
    if emit_x:
        out_specs = [row_blk(n)] + out_specs
        out_shape = [jax.ShapeDtypeStruct((m, n), F32)] + out_shape
    return pl.pallas_call(
        functools.partial(_matmul_residual_norm_kernel, n_a=len(a_list), emit_x=emit_x),
        grid=(m // tm,),
        in_specs=[row_blk(a.shape[1]) for a in a_list] + [
            pl.BlockSpec((k, n), lambda i: (0, 0), pipeline_mode=pl.Buffered(1)),
            row_blk(n),
            pl.BlockSpec((1, n), lambda i: (0, 0))],
        out_specs=out_specs,
        out_shape=out_shape,
        compiler_params=_cparams(("parallel",)),
        name=name,
    )(*a_list, w, res, g.reshape(1, n))


def _wres_matmul_kernel(x_ref, *refs, shift, w_transposed):
    if shift == 0:
        wa_ref, o_ref, w_scr = refs
    else:
        wa_ref, wb_ref, o_ref, w_scr = refs

    @pl.when(pl.program_id(1) == 0)
    def _():
        if shift == 0:
            w_scr[...] = wa_ref[...].astype(BF16)
        else:
            tn = w_scr.shape[0]
            w_scr[pl.ds(0, tn - shift), :] = wa_ref[pl.ds(shift, tn - shift), :].astype(BF16)
            w_scr[pl.ds(tn - shift, shift), :] = wb_ref[pl.ds(0, shift), :].astype(BF16)

    mm = _dot_nt if w_transposed else _dot
    o_ref[...] = mm(x_ref[...], w_scr[...]).astype(o_ref.dtype)


def wres_matmul(x, w, start, n_out, *, w_transposed, tm, tn, out_dtype, name):
    m, k = x.shape
    shift = start % tn
    base = start - shift
    assert n_out % tn == 0 and (shift == 0 or (w_transposed and shift % BF16_TILE_ROWS == 0 and shift <= TAIL_ROWS))
    assert start + n_out <= w.shape[0 if w_transposed else 1]
    if w_transposed:
        w_specs = [pl.BlockSpec((tn, k), lambda j, i: (base // tn + j, 0))]
        if shift:
            w_specs.append(pl.BlockSpec((TAIL_ROWS, k), lambda j, i: ((base + (j + 1) * tn) // TAIL_ROWS, 0)))
        scratch = pltpu.VMEM((tn, k), BF16)
    else:
        w_specs = [pl.BlockSpec((k, tn), lambda j, i: (0, base // tn + j))]
        scratch = pltpu.VMEM((k, tn), BF16)
    return pl.pallas_call(
        functools.partial(_wres_matmul_kernel, shift=shift, w_transposed=w_transposed),
        grid=(n_out // tn, m // tm),
        in_specs=[pl.BlockSpec((tm, k), lambda j, i: (i, 0))] + w_specs,
        out_specs=pl.BlockSpec((tm, tn), lambda j, i: (i, j)),
        out_shape=jax.ShapeDtypeStruct((m, n_out), out_dtype),
        scratch_shapes=[scratch],
        compiler_params=_cparams(("parallel", "arbitrary")),
        name=name,
    )(x, *([w] * len(w_specs)))


def _cumsum_rows(x):
    n = x.shape[0]
    row = lax.broadcasted_iota(jnp.int32, x.shape, 0)
    shift = 1
    while shift < n:
        x = x + jnp.where(row >= shift, pltpu.roll(x, shift, axis=0), 0.0)
        shift *= 2
    return x


def _norm_gate_prep_kernel(x_ref, g_ref, w_ref, bias_ref, alog_ref, xn_ref, col_ref, row_ref):
    w = w_ref[...].astype(BF16)
    neg_a = -jnp.exp(alog_ref[...])
    for c in range(x_ref.shape[1] // CHUNK):
        rows = pl.ds(c * CHUNK, CHUNK)
        xn = _rms(x_ref[0, rows, :], g_ref[...]).astype(BF16)
        xn_ref[0, rows, :] = xn
        v = _dot_nt(xn, w) + bias_ref[...]
        lane = lax.broadcasted_iota(jnp.int32, v.shape, 1)
        is_f = (lane >= GATE_F) & (lane < GATE_DT)
        is_dt = (lane >= GATE_DT) & (lane < GATE_CS)
        logf = -_softplus(-v)
        dt = _softplus(v)
        da = jnp.where(is_dt, dt * neg_a, 0.0)
        pre = jnp.where(is_f, logf, 0.0) + pltpu.roll(da, GATE_CS - GATE_DT, axis=1)
        cum = _cumsum_rows(pre) * LOG2E
        out = jnp.where(lane < GATE_F, v * LOG2E, jnp.where(is_f, cum, jnp.where(is_dt, dt, cum)))
        col_ref[0, rows, :] = out
        row_ref[0, :, rows] = out.T


def norm_gate_prep(x, g, w_gate_t, bias_row, alog_row):
    b, s, k = x.shape
    w = w_gate_t.shape[0]
    tr = NORM_GATE_ROWS
    return pl.pallas_call(
        _norm_gate_prep_kernel,
        grid=(b, s // tr),
        in_specs=[pl.BlockSpec((1, tr, k), lambda i, c: (i, c, 0)),
                  pl.BlockSpec((1, k), lambda i, c: (0, 0)),
                  pl.BlockSpec((w, k), lambda i, c: (0, 0)),
                  pl.BlockSpec((1, w), lambda i, c: (0, 0)),
                  pl.BlockSpec((1, w), lambda i, c: (0, 0))],
        out_specs=[pl.BlockSpec((1, tr, k), lambda i, c: (i, c, 0)),
                   pl.BlockSpec((1, tr, w), lambda i, c: (i, c, 0)),
                   pl.BlockSpec((1, w, tr), lambda i, c: (i, 0, c))],
        out_shape=[jax.ShapeDtypeStruct((b, s, k), BF16),
                   jax.ShapeDtypeStruct((b, s, w), F32),
                   jax.ShapeDtypeStruct((b, w, s), F32)],
        compiler_params=_cparams(("parallel", "parallel")),
        name="norm_gate_prep",
    )(x, g.reshape(1, k), w_gate_t, bias_row, alog_row)


def _lane_column(tile, idx):
    lane = lax.broadcasted_iota(jnp.int32, tile.shape, 1)
    return jnp.sum(jnp.where(lane == idx, tile, 0.0), axis=1, keepdims=True)


def _mlstm_kernel(q_ref, k_ref, v_ref, o_ref, gcol_ref, grow_ref, gain_ref, side_ref, out_ref, side_out_ref):
    side_out_ref[...] = side_ref[...].astype(BF16)
    h = pl.program_id(1)
    s_len = q_ref.shape[1]
    L = CHUNK
    scale = MLSTM_DQK ** -0.5
    ri = lax.broadcasted_iota(jnp.int32, (L, L), 0)
    ci = lax.broadcasted_iota(jnp.int32, (L, L), 1)
    causal = ci <= ri
    gain = gain_ref[0]

    def intra(c):
        rows = pl.ds(c * L, L)
        q = q_ref[0, rows, :] * scale
        k = k_ref[0, rows, :]
        qb = q.astype(BF16)
        kb = k.astype(BF16)
        vb = v_ref[0, rows, :].astype(BF16)
        gc = gcol_ref[0, rows, :]
        i_col = _lane_column(gc, GATE_I + h)
        b_col = _lane_column(gc, GATE_F + h)
        i_row = grow_ref[0, pl.ds(GATE_I + h, 1), rows]
        b_row = grow_ref[0, pl.ds(GATE_F + h, 1), rows]
        log_d = jnp.where(causal, b_col - (b_row - i_row), -jnp.inf)
        btot = b_col[L - 1:L, :]
        lw = btot - b_col + i_col
        return dict(q=q, k=k, qb=qb, vb=vb, b_col=b_col, log_d=log_d, qk=_dot_nt(qb, kb), btot=btot, lw=lw,
                    m_intra=jnp.max(log_d, axis=1, keepdims=True), lw_max=jnp.max(lw, axis=0, keepdims=True))

    def finish(c, t, carry):
        c_state, n_state, m_state = carry
        rows = pl.ds(c * L, L)
        log_inter = t["b_col"] + m_state
        m_row = jnp.maximum(log_inter, t["m_intra"])
        scores = t["qk"] * jnp.exp2(t["log_d"] - m_row)
        inter = jnp.exp2(log_inter - m_row)
        num = _dot(scores.astype(BF16), t["vb"]) + inter * _dot(t["qb"], c_state.astype(BF16))
        nq = jnp.sum(scores, axis=1, keepdims=True) + inter * jnp.sum(t["q"] * n_state, axis=1, keepdims=True)
        denom = jnp.maximum(jnp.abs(nq), jnp.exp2(-m_row))
        hm = num / denom
        y = _rms(hm, gain) * _sigmoid(o_ref[0, rows, :])
        out_ref[0, rows, :] = y.astype(out_ref.dtype)
        m_new = jnp.maximum(t["btot"] + m_state, t["lw_max"])
        decay = jnp.exp2(t["btot"] + m_state - m_new)
        wk = jnp.exp2(t["lw"] - m_new) * t["k"]
        c_state = decay * c_state + _dot(wk.T.astype(BF16), t["vb"])
        n_state = decay * n_state + jnp.sum(wk, axis=0, keepdims=True)
        return c_state, n_state, m_new

    carry = (jnp.zeros((MLSTM_DQK, MLSTM_DV), F32), jnp.zeros((1, MLSTM_DQK), F32), jnp.zeros((1, 1), F32))
    n_chunks = s_len // L
    nxt = intra(0)
    for c in range(n_chunks):
        cur = nxt
        if c + 1 < n_chunks:
            nxt = intra(c + 1)
        carry = finish(c, cur, carry)


def mlstm(proj, gcol, grow, gain, w_side):
    b, s, _ = proj.shape
    grid = (b, MLSTM_HEADS)
    side_in, side_out, side_shape = _side_cast_specs(w_side, grid)
    qk_blk = lambda off: pl.BlockSpec((1, s, MLSTM_DQK), lambda i, h: (i, 0, off // MLSTM_DQK + h))
    v_blk = lambda off: pl.BlockSpec((1, s, MLSTM_DV), lambda i, h: (i, 0, off // MLSTM_DV + h))
    return pl.pallas_call(
        _mlstm_kernel,
        grid=grid,
        in_specs=[qk_blk(OFF_Q), qk_blk(OFF_K), v_blk(OFF_V), v_blk(OFF_O),
                  pl.BlockSpec((1, s, GATE_W), lambda i, h: (i, 0, 0)),
                  pl.BlockSpec((1, GATE_W, s), lambda i, h: (i, 0, 0)),
                  pl.BlockSpec((1, 1, MLSTM_DV), lambda i, h: (h, 0, 0)),
                  side_in],
        out_specs=[pl.BlockSpec((1, s, MLSTM_DV), lambda i, h: (i, 0, h)), side_out],
        out_shape=[jax.ShapeDtypeStruct((b, s, MLSTM_HEADS * MLSTM_DV), BF16), side_shape],
        compiler_params=_cparams(("parallel", "parallel")),
        name="mlstm",
    )(proj, proj, proj, proj, gcol, grow, gain.reshape(MLSTM_HEADS, 1, MLSTM_DV), w_side)


def _conv_silu(ref, c, w_ref, b_ref):
    L = SSD_CHUNK
    cur = ref[0, pl.ds(c * L, L), :]
    if c == 0:
        halo = jnp.zeros((CONV_HALO, cur.shape[1]), F32)
    else:
        halo = ref[0, pl.ds(c * L - CONV_HALO, CONV_HALO), :]
    ext = jnp.concatenate([halo, cur], axis=0)
    assert SSD_CONV == 4
    w = w_ref[...]
    prev = pltpu.roll(ext, 1, axis=0)
    u = w[1:2, :] * ext + w[0:1, :] * prev
    acc = (b_ref[...] + w[3:4, :] * cur + w[2:3, :] * prev[CONV_HALO:, :]) + pltpu.roll(u, 2, axis=0)[CONV_HALO:, :]
    return _silu(acc)


def _per_head_lanes(vals, shape):
    lane = lax.broadcasted_iota(jnp.int32, shape, 1)
    out = jnp.broadcast_to(vals[SSD_HEADS_PER_GROUP - 1], shape)
    for r in range(SSD_HEADS_PER_GROUP - 2, -1, -1):
        out = jnp.where(lane < (r + 1) * SSD_HEADDIM, vals[r], out)
    return out


def _ssd_kernel(x_ref, b_ref, c_ref, z_ref, wx_ref, wb_ref, wc_ref, bx_ref, bb_ref, bc_ref,
                gcol_ref, grow_ref, d_ref, gain_ref, out_ref):
    g = pl.program_id(1)
    s_len = x_ref.shape[1]
    L = SSD_CHUNK
    R = SSD_HEADS_PER_GROUP
    ri = lax.broadcasted_iota(jnp.int32, (L, L), 0)
    ci = lax.broadcasted_iota(jnp.int32, (L, L), 1)
    causal = ci <= ri
    lane_head = lax.broadcasted_iota(jnp.int32, (L, SSD_GROUP_W), 1) // SSD_HEADDIM

    state = jnp.zeros((SSD_STATE, SSD_GROUP_W), F32)
    cs_prev_end = None

    for c in range(s_len // L):
        rows = pl.ds(c * L, L)
        xs = _conv_silu(x_ref, c, wx_ref, bx_ref)
        bm = _conv_silu(b_ref, c, wb_ref, bb_ref)
        cm = _conv_silu(c_ref, c, wc_ref, bc_ref)
        bmb = bm.astype(BF16)
        cmb = cm.astype(BF16)
        gc = gcol_ref[0, rows, :]
        dt_cols = [_lane_column(gc, GATE_DT + R * g + r) for r in range(R)]
        cs_cols = [_lane_column(gc, GATE_CS + R * g + r) for r in range(R)]
        base, half = (c * L) // CHUNK * CHUNK, (c * L) % CHUNK
        cs_rows = [grow_ref[0, pl.ds(GATE_CS + R * g + r, 1), pl.ds(base, CHUNK)][:, half:half + L]
                   for r in range(R)]
        cs_raw_end = [col[L - 1:L, :] for col in cs_cols]
        if half:
            cs_cols = [col - off for col, off in zip(cs_cols, cs_prev_end)]
            cs_rows = [row - off for row, off in zip(cs_rows, cs_prev_end)]
        cs_prev_end = cs_raw_end

        dt_full = _per_head_lanes(dt_cols, (L, SSD_GROUP_W))
        cs_full = _per_head_lanes(cs_cols, (L, SSD_GROUP_W))
        cs_end = _per_head_lanes([col[L - 1:L, :] for col in cs_cols], (1, SSD_GROUP_W))

        xd = xs * dt_full
        xdb = xd.astype(BF16)
        cb = _dot_nt(cmb, bmb)
        y = _dot(cmb, state.astype(BF16)) * jnp.exp2(cs_full)
        for r in range(R):
            lmat = jnp.where(causal, jnp.exp2(cs_cols[r] - cs_rows[r]), 0.0)
            xr = jnp.where(lane_head == r, xdb, jnp.zeros_like(xdb))
            y = y + _dot((cb * lmat).astype(BF16), xr)
        y = y + d_ref[...] * xs
        y = y * _silu(z_ref[0, rows, :])
        out_ref[0, rows, :] = _rms(y, gain_ref[...]).astype(out_ref.dtype)

        decay_end = jnp.exp2(cs_end - cs_full)
        state = jnp.exp2(cs_end) * state + _dot(bm.T.astype(BF16), (xd * decay_end).astype(BF16))


def ssd(proj, conv_w, conv_b, gcol, grow, d_row, gain_row):
    b, s, _ = proj.shape
    G = SSD_GROUPS
    wide = lambda off: pl.BlockSpec((1, s, SSD_GROUP_W), lambda i, g: (i, 0, off // SSD_GROUP_W + g))
    narrow = lambda off: pl.BlockSpec((1, s, SSD_STATE), lambda i, g: (i, 0, off // SSD_STATE + g))
    cw = lambda rows, width, off: pl.BlockSpec((rows, width), lambda i, g: (0, off // width + g))
    return pl.pallas_call(
        _ssd_kernel,
        grid=(b, G),
        in_specs=[wide(OFF_X), narrow(OFF_B), narrow(OFF_C), wide(OFF_Z),
                  cw(SSD_CONV, SSD_GROUP_W, 0), cw(SSD_CONV, SSD_STATE, 2048), cw(SSD_CONV, SSD_STATE, 3072),
                  cw(1, SSD_GROUP_W, 0), cw(1, SSD_STATE, 2048), cw(1, SSD_STATE, 3072),
                  pl.BlockSpec((1, s, GATE_W), lambda i, g: (i, 0, 0)),
                  pl.BlockSpec((1, GATE_W, s), lambda i, g: (i, 0, 0)),
                  cw(1, SSD_GROUP_W, 0), cw(1, SSD_GROUP_W, 0)],
        out_specs=pl.BlockSpec((1, s, SSD_GROUP_W), lambda i, g: (i, 0, g)),
        out_shape=jax.ShapeDtypeStruct((b, s, SSD_D_INNER), BF16),
        compiler_params=_cparams(("parallel", "parallel")),
        name="ssd",
    )(proj, proj, proj, proj, conv_w, conv_w, conv_w, conv_b, conv_b, conv_b, gcol, grow, d_row, gain_row)


def _moba_kernel(q_ref, k_ref, v_ref, cos_ref, sin_ref, side_ref, out_ref, side_out_ref,
                 qs_ref, qb_ref, kb_ref, vt_ref, s_ref, pb_ref):
    side_out_ref[...] = side_ref[...].astype(BF16)
    s_len = q_ref.shape[1]
    BLK = MOBA_BLOCK
    nb = s_len // BLK
    nbp = -(-nb // 8) * 8
    half = ATTN_HEAD_DIM // 2
    scale = ATTN_HEAD_DIM ** -0.5
    cos = cos_ref[...]
    sin = sin_ref[...]

    def rope(x):
        return x * cos + pltpu.roll(x, half, axis=1) * sin

    q = rope(q_ref[0])
    k = rope(k_ref[0])
    qs_ref[...] = q
    qb_ref[...] = (q * (scale * LOG2E)).astype(BF16)
    kb_ref[...] = k.astype(BF16)
    vt_ref[...] = v_ref[0].T.astype(BF16)

    row8 = lax.broadcasted_iota(jnp.int32, (nbp, ATTN_HEAD_DIM), 0)
    k_mean = jnp.zeros((nbp, ATTN_HEAD_DIM), F32)
    for j in range(nb):
        k_mean = jnp.where(row8 == j, jnp.mean(k[j * BLK:(j + 1) * BLK, :], axis=0, keepdims=True), k_mean)

    SUB = 8
    GRP = BLK // SUB
    gi = lax.broadcasted_iota(jnp.int32, (GRP, SUB, BLK), 0)
    si = lax.broadcasted_iota(jnp.int32, (GRP, SUB, BLK), 1)
    ci = lax.broadcasted_iota(jnp.int32, (GRP, SUB, BLK), 2)
    causal_bias = jnp.where(gi * SUB + si <= ci, 0.0, -jnp.inf)
    blk_row = lax.broadcasted_iota(jnp.int32, (nbp, BLK), 0)

    def scores(i):
        qrows = pl.ds(i * BLK, BLK)
        keep = None
        if i > MOBA_TOPK:
            gate = lax.dot_general(k_mean, qs_ref[qrows, :], (((1,), (1,)), ((), ())),
                                   precision=lax.Precision.HIGHEST, preferred_element_type=F32)
            rank = jnp.zeros((nbp, BLK), F32)
            for m in range(i):
                gm = gate[m:m + 1, :]
                ahead = (gm > gate) | ((gm == gate) & (blk_row > m))
                rank = rank + jnp.where(ahead, 1.0, 0.0)
            keep = rank < float(MOBA_TOPK)
        qb = qb_ref[qrows, :]
        m8 = None
        for j in range(i + 1):
            krows = pl.ds(j * BLK, BLK)
            sj = _dot_nt(kb_ref[krows, :], qb).reshape(GRP, SUB, BLK)
            if j == i:
                sj = sj + causal_bias
            elif keep is not None:
                sj = sj + jnp.broadcast_to(jnp.where(keep[j:j + 1, :], 0.0, -jnp.inf), (SUB, BLK))[None]
            s_ref[i % 2, krows, :] = sj.reshape(BLK, BLK)
            mj = jnp.max(sj, axis=0)
            m8 = mj if m8 is None else jnp.maximum(m8, mj)
        return m8

    def attend(i, m8):
        qrows = pl.ds(i * BLK, BLK)
        nk = (i + 1) * BLK
        m_row = jnp.broadcast_to(jnp.max(m8, axis=0, keepdims=True), (SUB, BLK))
        den8 = jnp.zeros((SUB, BLK), F32)
        for j in range(i + 1):
            krows = pl.ds(j * BLK, BLK)
            p = jnp.exp2(s_ref[i % 2, krows, :].reshape(GRP, SUB, BLK) - m_row[None])
            den8 = den8 + jnp.sum(p, axis=0)
            pb_ref[i % 2, krows, :] = p.reshape(BLK, BLK).astype(BF16)
        den = jnp.sum(den8, axis=0, keepdims=True)
        ot = _dot(vt_ref[:, pl.ds(0, nk)], pb_ref[i % 2, pl.ds(0, nk), :])
        out_ref[0, qrows, :] = (ot / den).T.astype(out_ref.dtype)

    m8_next = scores(0)
    for i in range(nb):
        m8_cur = m8_next
        if i + 1 < nb:
            m8_next = scores(i + 1)
        attend(i, m8_cur)


def moba(qkv, cos_full, sin_signed, w_side):
    b, s, _ = qkv.shape
    hd = ATTN_HEAD_DIM
    grid = (b, ATTN_HEADS)
    side_in, side_out, side_shape = _side_cast_specs(w_side, grid)
    blk = lambda part: pl.BlockSpec((1, s, hd), lambda i, h: (i, 0, part * ATTN_HEADS + h))
    return pl.pallas_call(
        _moba_kernel,
        grid=grid,
        in_specs=[blk(0), blk(1), blk(2),
                  pl.BlockSpec((s, hd), lambda i, h: (0, 0)),
                  pl.BlockSpec((s, hd), lambda i, h: (0, 0)),
                  side_in],
        out_specs=[pl.BlockSpec((1, s, hd), lambda i, h: (i, 0, h)), side_out],
        out_shape=[jax.ShapeDtypeStruct((b, s, ATTN_HEADS * hd), BF16), side_shape],
        scratch_shapes=[pltpu.VMEM((s, hd), F32), pltpu.VMEM((s, hd), BF16), pltpu.VMEM((s, hd), BF16),
                        pltpu.VMEM((hd, s), BF16), pltpu.VMEM((2, s, MOBA_BLOCK), F32),
                        pltpu.VMEM((2, s, MOBA_BLOCK), BF16)],
        compiler_params=_cparams(("parallel", "parallel")),
        name="moba",
    )(qkv, qkv, qkv, cos_full, sin_signed, w_side)


def _rope_tables(s):
    half = ATTN_HEAD_DIM // 2
    inv = ROPE_THETA ** (-jnp.arange(half, dtype=F32) / half)
    ang = jnp.arange(s).astype(F32)[:, None] * inv[None, :]
    cos, sin = jnp.cos(ang), jnp.sin(ang)
    return jnp.concatenate([cos, cos], axis=-1), jnp.concatenate([-sin, sin], axis=-1)


def kernel(x, l0_norm_mix, l0_w_in, l0_mlstm_gate_bias, l0_mlstm_norm, l0_ssd_conv_w, l0_ssd_conv_b,
           l0_ssd_dt_bias, l0_ssd_a_log, l0_ssd_d, l0_ssd_norm, l0_w_out, l0_norm_ffn, l0_ffn_gate,
           l0_ffn_up, l0_ffn_down, l1_norm_mix, l1_w_qkv, l1_w_o, l1_norm_ffn, l1_ffn_gate, l1_ffn_up,
           l1_ffn_down, final_norm):
    b, s, d = x.shape
    m = b * s
    assert d == D_MODEL and s % CHUNK == 0 and s % MOBA_BLOCK == 0 and m % 1024 == 0
    xr = x.reshape(m, d)

    w_in_t = l0_w_in.T
    w_gate_t = jnp.concatenate([w_in_t[IN_QKVO_END:IN_ZX_START], w_in_t[IN_ZX_END:IN_END],
                                jnp.zeros((GATE_W - GATE_COLS, d), F32)], axis=0)
    pad = jnp.zeros((GATE_W - GATE_COLS,), F32)
    bias_row = jnp.concatenate([l0_mlstm_gate_bias, l0_ssd_dt_bias, pad]).reshape(1, GATE_W)
    alog_row = jnp.concatenate([jnp.zeros((GATE_DT,), F32), l0_ssd_a_log, pad]).reshape(1, GATE_W)
    xn, gcol, grow = norm_gate_prep(x, l0_norm_mix, w_gate_t, bias_row, alog_row)
    xn = xn.reshape(m, d)
    proj_a = wres_matmul(xn, w_in_t, 0, IN_QKVO_END, w_transposed=True, tm=1024, tn=1024, out_dtype=F32,
                         name="in_proj_qkvo").reshape(b, s, -1)
    proj_b = wres_matmul(xn, w_in_t, IN_ZX_START, IN_ZX_END - IN_ZX_START, w_transposed=True, tm=1024, tn=1024,
                         out_dtype=F32, name="in_proj_zx").reshape(b, s, -1)

    hm, w_out_bf16 = mlstm(proj_a, gcol, grow, l0_mlstm_norm, l0_w_out)
    hm = hm.reshape(m, D_MODEL)
    d_row = jnp.repeat(l0_ssd_d, SSD_HEADDIM).reshape(1, SSD_D_INNER)
    ys = ssd(proj_b, l0_ssd_conv_w, l0_ssd_conv_b.reshape(1, -1), gcol, grow, d_row,
             l0_ssd_norm.reshape(1, SSD_D_INNER)).reshape(m, SSD_D_INNER)
    xr, xn = matmul_residual_norm([hm, ys], w_out_bf16, xr, l0_norm_ffn, tm=512, emit_x=True,
                                  norm_dtype=BF16, name="l0_out_proj")
    hff, w_down_bf16 = swiglu_up(xn, l0_ffn_gate, l0_ffn_up, l0_ffn_down, tm=1024, tn=512, name="l0_ffn_up")
    xr, xn = matmul_residual_norm([hff], w_down_bf16, xr, l1_norm_mix, tm=256, emit_x=True,
                                  norm_dtype=BF16, name="l0_ffn_down")

    qkv = wres_matmul(xn, l1_w_qkv, 0, 3 * D_MODEL, w_transposed=False, tm=1024, tn=1024, out_dtype=F32,
                      name="qkv_proj").reshape(b, s, 3 * D_MODEL)
    cos_full, sin_signed = _rope_tables(s)
    att, w_o_bf16 = moba(qkv, cos_full, sin_signed, l1_w_o)
    xr, xn = matmul_residual_norm([att.reshape(m, D_MODEL)], w_o_bf16, xr, l1_norm_ffn, tm=512, emit_x=True,
                                  norm_dtype=BF16, name="l1_out_proj")
    hff, w_down_bf16 = swiglu_up(xn, l1_ffn_gate, l1_ffn_up, l1_ffn_down, tm=1024, tn=512, name="l1_ffn_up")
    (out,) = matmul_residual_norm([hff], w_down_bf16, xr, final_norm, tm=256, emit_x=False,
                                  norm_dtype=F32, name="l1_ffn_down")
    return out.reshape(b, s, d)
```

```python
import functools

import jax
import jax.numpy as jnp
from jax import lax
from jax.experimental import pallas as pl
from jax.experimental.pallas import tpu as pltpu

F32 = jnp.float32
BF16 = jnp.bfloat16

D_MODEL = 2048
NORM_EPS = 1e-6
MLSTM_HEADS = 8
MLSTM_DQK = 128
MLSTM_DV = 256
SSD_D_INNER = 2048
SSD_HEADDIM = 64
SSD_HEADS = 32
SSD_STATE = 128
SSD_GROUPS = 8
SSD_HEADS_PER_GROUP = SSD_HEADS // SSD_GROUPS
SSD_GROUP_W = SSD_HEADS_PER_GROUP * SSD_HEADDIM
SSD_CONV = 4
ATTN_HEADS = 16
ATTN_HEAD_DIM = 128
MOBA_BLOCK = 256
MOBA_TOPK = 3
ROPE_THETA = 10000.0
LOG2E = 1.4426950408889634

MLSTM_QK_W = MLSTM_HEADS * MLSTM_DQK
MLSTM_V_W = MLSTM_HEADS * MLSTM_DV
SSD_BC_W = SSD_GROUPS * SSD_STATE
IN_QKVO_END = 2 * MLSTM_QK_W + 2 * MLSTM_V_W
IN_ZX_START = IN_QKVO_END + 2 * MLSTM_HEADS
IN_ZX_END = IN_ZX_START + 2 * SSD_D_INNER + 2 * SSD_BC_W
IN_END = IN_ZX_END + SSD_HEADS
OFF_Q, OFF_K, OFF_V, OFF_O = 0, MLSTM_QK_W, 2 * MLSTM_QK_W, 2 * MLSTM_QK_W + MLSTM_V_W
OFF_Z, OFF_X, OFF_B, OFF_C = 0, SSD_D_INNER, 2 * SSD_D_INNER, 2 * SSD_D_INNER + SSD_BC_W
CONV_OFF_X, CONV_OFF_B, CONV_OFF_C = 0, SSD_D_INNER, SSD_D_INNER + SSD_BC_W
GATE_I, GATE_F, GATE_DT, GATE_CS = 0, 8, 16, 48
GATE_COLS = 48
GATE_W = 128

CHUNK = 256
SSD_CHUNK = 128
NORM_GATE_ROWS = 1024
SUBLANES = 8
CONV_HALO = SUBLANES
BF16_TILE_ROWS = 16
TAIL_ROWS = 128
WCAST_ROWS = 256

V7X_VMEM_LIMIT_BYTES = 56 * 1024 * 1024

PROJ_TM, PROJ_TN = 1024, 1024
FFN_UP_TM, FFN_UP_TN = 1024, 512
OUT_PROJ_TM = 512
FFN_DOWN_TM = 256


def _cparams(sem):
    return pltpu.CompilerParams(dimension_semantics=sem, vmem_limit_bytes=V7X_VMEM_LIMIT_BYTES)


def _sigmoid(x):
    return 0.5 * jnp.tanh(0.5 * x) + 0.5


def _silu(x):
    h = 0.5 * x
    return h * jnp.tanh(h) + h


def _softplus(x):
    return jnp.maximum(x, 0.0) + jnp.log(1.0 + jnp.exp(-jnp.abs(x)))


def _rms(x, g):
    ms = jnp.mean(x * x, axis=-1, keepdims=True)
    return x * lax.rsqrt(ms + NORM_EPS) * g


def _dot(a, b):
    return jnp.dot(a, b, preferred_element_type=F32)


def _dot_nt(a, b):
    return lax.dot_general(a, b, (((1,), (1,)), ((), ())), preferred_element_type=F32)


def _side_cast_specs(w, grid):
    n_steps = grid[0] * grid[1]
    rows, cols = w.shape
    assert rows % n_steps == 0 and (rows // n_steps) % BF16_TILE_ROWS == 0
    blk = pl.BlockSpec((rows // n_steps, cols), lambda a, b_: (a * grid[1] + b_, 0))
    return blk, blk, jax.ShapeDtypeStruct((rows, cols), BF16)


def _swiglu_up_kernel(x_ref, wg_ref, wu_ref, side_ref, o_ref, side_out_ref, wg_scr, wu_scr):
    side_out_ref[...] = side_ref[...].astype(BF16)

    @pl.when(pl.program_id(1) == 0)
    def _():
        for r in range(0, wg_scr.shape[0], WCAST_ROWS):
            rows = pl.ds(r, WCAST_ROWS)
            wg_scr[rows, :] = wg_ref[rows, :].astype(BF16)
            wu_scr[rows, :] = wu_ref[rows, :].astype(BF16)

    x = x_ref[...]
    gate = _dot(x, wg_scr[...])
    up = _dot(x, wu_scr[...])
    o_ref[...] = (_silu(gate) * up).astype(o_ref.dtype)


def swiglu_up(xn, wg, wu, w_side, *, tm, tn, name):
    m, k = xn.shape
    n = wg.shape[1]
    grid = (n // tn, m // tm)
    side_in, side_out, side_shape = _side_cast_specs(w_side, grid)
    return pl.pallas_call(
        _swiglu_up_kernel,
        grid=grid,
        in_specs=[pl.BlockSpec((tm, k), lambda j, i: (i, 0)),
                  pl.BlockSpec((k, tn), lambda j, i: (0, j)),
                  pl.BlockSpec((k, tn), lambda j, i: (0, j)),
                  side_in],
        out_specs=[pl.BlockSpec((tm, tn), lambda j, i: (i, j)), side_out],
        out_shape=[jax.ShapeDtypeStruct((m, n), BF16), side_shape],
        scratch_shapes=[pltpu.VMEM((k, tn), BF16), pltpu.VMEM((k, tn), BF16)],
        compiler_params=_cparams(("parallel", "arbitrary")),
        name=name,
    )(xn, wg, wu, w_side)


def _matmul_residual_norm_kernel(*refs, n_a, emit_x):
    a_refs = refs[:n_a]
    w_ref, r_ref, g_ref = refs[n_a:n_a + 3]
    outs = refs[n_a + 3:]
    acc = r_ref[...]
    off = 0
    for a_ref in a_refs:
        kk = a_ref.shape[1]
        acc = acc + _dot(a_ref[...], w_ref[pl.ds(off, kk), :])
        off += kk
    if emit_x:
        outs[0][...] = acc
    xn_ref = outs[-1]
    xn_ref[...] = _rms(acc, g_ref[...]).astype(xn_ref.dtype)


def matmul_residual_norm(a_list, w, res, g, *, tm, emit_x, norm_dtype, name):
    m, n = res.shape
    k = w.shape[0]
    assert sum(a.shape[1] for a in a_list) == k
    row_blk = lambda width: pl.BlockSpec((tm, width), lambda i: (i, 0))
    out_specs = [row_blk(n)]
    out_shape = [jax.ShapeDtypeStruct((m, n), norm_dtype)]
    if emit_x:
        out_specs = [row_blk(n)] + out_specs
        out_shape = [jax.ShapeDtypeStruct((m, n), F32)] + out_shape
    return pl.pallas_call(
        functools.partial(_matmul_residual_norm_kernel, n_a=len(a_list), emit_x=emit_x),
        grid=(m // tm,),
        in_specs=[row_blk(a.shape[1]) for a in a_list] + [
            pl.BlockSpec((k, n), lambda i: (0, 0), pipeline_mode=pl.Buffered(1)),
            row_blk(n),
            pl.BlockSpec((1, n), lambda i: (0, 0))],
        out_specs=out_specs,
        out_shape=out_shape,
        compiler_params=_cparams(("parallel",)),
        name=name,
    )(*a_list, w, res, g.reshape(1, n))


def _wres_matmul_kernel(x_ref, *refs, shift, w_transposed):
    if shift == 0:
        wa_ref, o_ref, w_scr = refs
    else:
        wa_ref, wb_ref, o_ref, w_scr = refs

    @pl.when(pl.program_id(1) == 0)
    def _():
        if shift == 0:
            w_scr[...] = wa_ref[...].astype(BF16)
        else:
            tn = w_scr.shape[0]
            w_scr[pl.ds(0, tn - shift), :] = wa_ref[pl.ds(shift, tn - shift), :].astype(BF16)
            w_scr[pl.ds(tn - shift, shift), :] = wb_ref[pl.ds(0, shift), :].astype(BF16)

    mm = _dot_nt if w_transposed else _dot
    o_ref[...] = mm(x_ref[...], w_scr[...]).astype(o_ref.dtype)


def wres_matmul(x, w, start, n_out, *, w_transposed, tm, tn, out_dtype, name):
    m, k = x.shape
    shift = start % tn
    base = start - shift
    assert n_out % tn == 0 and (shift == 0 or (w_transposed and shift % BF16_TILE_ROWS == 0 and shift <= TAIL_ROWS))
    assert start + n_out <= w.shape[0 if w_transposed else 1]
    if w_transposed:
        w_specs = [pl.BlockSpec((tn, k), lambda j, i: (base // tn + j, 0))]
        if shift:
            w_specs.append(pl.BlockSpec((TAIL_ROWS, k), lambda j, i: ((base + (j + 1) * tn) // TAIL_ROWS, 0)))
        scratch = pltpu.VMEM((tn, k), BF16)
    else:
        w_specs = [pl.BlockSpec((k, tn), lambda j, i: (0, base // tn + j))]
        scratch = pltpu.VMEM((k, tn), BF16)
    return pl.pallas_call(
        functools.partial(_wres_matmul_kernel, shift=shift, w_transposed=w_transposed),
        grid=(n_out // tn, m // tm),
        in_specs=[pl.BlockSpec((tm, k), lambda j, i: (i, 0))] + w_specs,
        out_specs=pl.BlockSpec((tm, tn), lambda j, i: (i, j)),
        out_shape=jax.ShapeDtypeStruct((m, n_out), out_dtype),
        scratch_shapes=[scratch],
        compiler_params=_cparams(("parallel", "arbitrary")),
        name=name,
    )(x, *([w] * len(w_specs)))


def _cumsum_rows(x):
    n = x.shape[0]
    row = lax.broadcasted_iota(jnp.int32, x.shape, 0)
    shift = 1
    while shift < n:
        x = x + jnp.where(row >= shift, pltpu.roll(x, shift, axis=0), 0.0)
        shift *= 2
    return x


def _norm_gate_prep_kernel(x_ref, g_ref, w_ref, bias_ref, alog_ref, xn_ref, col_ref, row_ref):
    w = w_ref[...].astype(BF16)
    neg_a = -jnp.exp(alog_ref[...])
    for c in range(x_ref.shape[1] // CHUNK):
        rows = pl.ds(c * CHUNK, CHUNK)
        xn = _rms(x_ref[0, rows, :], g_ref[...]).astype(BF16)
        xn_ref[0, rows, :] = xn
        v = _dot_nt(xn, w) + bias_ref[...]
        lane = lax.broadcasted_iota(jnp.int32, v.shape, 1)
        is_f = (lane >= GATE_F) & (lane < GATE_DT)
        is_dt = (lane >= GATE_DT) & (lane < GATE_CS)
        logf = -_softplus(-v)
        dt = _softplus(v)
        da = jnp.where(is_dt, dt * neg_a, 0.0)
        pre = jnp.where(is_f, logf, 0.0) + pltpu.roll(da, GATE_CS - GATE_DT, axis=1)
        cum = _cumsum_rows(pre) * LOG2E
        out = jnp.where(lane < GATE_F, v * LOG2E, jnp.where(is_f, cum, jnp.where(is_dt, dt, cum)))
        col_ref[0, rows, :] = out
        row_ref[0, :, rows] = out.T


def norm_gate_prep(x, g, w_gate_t, bias_row, alog_row):
    b, s, k = x.shape
    w = w_gate_t.shape[0]
    tr = NORM_GATE_ROWS
    return pl.pallas_call(
        _norm_gate_prep_kernel,
        grid=(b, s // tr),
        in_specs=[pl.BlockSpec((1, tr, k), lambda i, c: (i, c, 0)),
                  pl.BlockSpec((1, k), lambda i, c: (0, 0)),
                  pl.BlockSpec((w, k), lambda i, c: (0, 0)),
                  pl.BlockSpec((1, w), lambda i, c: (0, 0)),
                  pl.BlockSpec((1, w), lambda i, c: (0, 0))],
        out_specs=[pl.BlockSpec((1, tr, k), lambda i, c: (i, c, 0)),
                   pl.BlockSpec((1, tr, w), lambda i, c: (i, c, 0)),
                   pl.BlockSpec((1, w, tr), lambda i, c: (i, 0, c))],
        out_shape=[jax.ShapeDtypeStruct((b, s, k), BF16),
                   jax.ShapeDtypeStruct((b, s, w), F32),
                   jax.ShapeDtypeStruct((b, w, s), F32)],
        compiler_params=_cparams(("parallel", "parallel")),
        name="norm_gate_prep",
    )(x, g.reshape(1, k), w_gate_t, bias_row, alog_row)


def _lane_column(tile, idx):
    lane = lax.broadcasted_iota(jnp.int32, tile.shape, 1)
    return jnp.sum(jnp.where(lane == idx, tile, 0.0), axis=1, keepdims=True)


def _mlstm_kernel(q_ref, k_ref, v_ref, o_ref, gcol_ref, grow_ref, gain_ref, side_ref, out_ref, side_out_ref):
    side_out_ref[...] = side_ref[...].astype(BF16)
    h = pl.program_id(1)
    s_len = q_ref.shape[1]
    L = CHUNK
    scale = MLSTM_DQK ** -0.5
    ri = lax.broadcasted_iota(jnp.int32, (L, L), 0)
    ci = lax.broadcasted_iota(jnp.int32, (L, L), 1)
    causal = ci <= ri
    gain = gain_ref[0]

    def intra(c):
        rows = pl.ds(c * L, L)
        q = q_ref[0, rows, :] * scale
        k = k_ref[0, rows, :]
        qb = q.astype(BF16)
        kb = k.astype(BF16)
        vb = v_ref[0, rows, :].astype(BF16)
        gc = gcol_ref[0, rows, :]
        i_col = _lane_column(gc, GATE_I + h)
        b_col = _lane_column(gc, GATE_F + h)
        i_row = grow_ref[0, pl.ds(GATE_I + h, 1), rows]
        b_row = grow_ref[0, pl.ds(GATE_F + h, 1), rows]
        log_d = jnp.where(causal, b_col - (b_row - i_row), -jnp.inf)
        btot = b_col[L - 1:L, :]
        lw = btot - b_col + i_col
        return dict(q=q, k=k, qb=qb, vb=vb, b_col=b_col, log_d=log_d, qk=_dot_nt(qb, kb), btot=btot, lw=lw,
                    m_intra=jnp.max(log_d, axis=1, keepdims=True), lw_max=jnp.max(lw, axis=0, keepdims=True))

    def finish(c, t, carry):
        c_state, n_state, m_state = carry
        rows = pl.ds(c * L, L)
        log_inter = t["b_col"] + m_state
        m_row = jnp.maximum(log_inter, t["m_intra"])
        scores = t["qk"] * jnp.exp2(t["log_d"] - m_row)
        inter = jnp.exp2(log_inter - m_row)
        num = _dot(scores.astype(BF16), t["vb"]) + inter * _dot(t["qb"], c_state.astype(BF16))
        nq = jnp.sum(scores, axis=1, keepdims=True) + inter * jnp.sum(t["q"] * n_state, axis=1, keepdims=True)
        denom = jnp.maximum(jnp.abs(nq), jnp.exp2(-m_row))
        hm = num / denom
        y = _rms(hm, gain) * _sigmoid(o_ref[0, rows, :])
        out_ref[0, rows, :] = y.astype(out_ref.dtype)
        m_new = jnp.maximum(t["btot"] + m_state, t["lw_max"])
        decay = jnp.exp2(t["btot"] + m_state - m_new)
        wk = jnp.exp2(t["lw"] - m_new) * t["k"]
        c_state = decay * c_state + _dot(wk.T.astype(BF16), t["vb"])
        n_state = decay * n_state + jnp.sum(wk, axis=0, keepdims=True)
        return c_state, n_state, m_new

    carry = (jnp.zeros((MLSTM_DQK, MLSTM_DV), F32), jnp.zeros((1, MLSTM_DQK), F32), jnp.zeros((1, 1), F32))
    n_chunks = s_len // L
    nxt = intra(0)
    for c in range(n_chunks):
        cur = nxt
        if c + 1 < n_chunks:
            nxt = intra(c + 1)
        carry = finish(c, cur, carry)


def mlstm(proj, gcol, grow, gain, w_side):
    b, s, _ = proj.shape
    grid = (b, MLSTM_HEADS)
    side_in, side_out, side_shape = _side_cast_specs(w_side, grid)
    qk_blk = lambda off: pl.BlockSpec((1, s, MLSTM_DQK), lambda i, h: (i, 0, off // MLSTM_DQK + h))
    v_blk = lambda off: pl.BlockSpec((1, s, MLSTM_DV), lambda i, h: (i, 0, off // MLSTM_DV + h))
    return pl.pallas_call(
        _mlstm_kernel,
        grid=grid,
        in_specs=[qk_blk(OFF_Q), qk_blk(OFF_K), v_blk(OFF_V), v_blk(OFF_O),
                  pl.BlockSpec((1, s, GATE_W), lambda i, h: (i, 0, 0)),
                  pl.BlockSpec((1, GATE_W, s), lambda i, h: (i, 0, 0)),
                  pl.BlockSpec((1, 1, MLSTM_DV), lambda i, h: (h, 0, 0)),
                  side_in],
        out_specs=[pl.BlockSpec((1, s, MLSTM_DV), lambda i, h: (i, 0, h)), side_out],
        out_shape=[jax.ShapeDtypeStruct((b, s, MLSTM_HEADS * MLSTM_DV), BF16), side_shape],
        compiler_params=_cparams(("parallel", "parallel")),
        name="mlstm",
    )(proj, proj, proj, proj, gcol, grow, gain.reshape(MLSTM_HEADS, 1, MLSTM_DV), w_side)


def _conv_silu(ref, c, w_ref, b_ref):
    L = SSD_CHUNK
    cur = ref[0, pl.ds(c * L, L), :]
    if c == 0:
        halo = jnp.zeros((CONV_HALO, cur.shape[1]), F32)
    else:
        halo = ref[0, pl.ds(c * L - CONV_HALO, CONV_HALO), :]
    ext = jnp.concatenate([halo, cur], axis=0)
    assert SSD_CONV == 4
    w = w_ref[...]
    prev = pltpu.roll(ext, 1, axis=0)
    u = w[1:2, :] * ext + w[0:1, :] * prev
    acc = (b_ref[...] + w[3:4, :] * cur + w[2:3, :] * prev[CONV_HALO:, :]) + pltpu.roll(u, 2, axis=0)[CONV_HALO:, :]
    return _silu(acc)


def _per_head_lanes(vals, shape):
    lane = lax.broadcasted_iota(jnp.int32, shape, 1)
    out = jnp.broadcast_to(vals[SSD_HEADS_PER_GROUP - 1], shape)
    for r in range(SSD_HEADS_PER_GROUP - 2, -1, -1):
        out = jnp.where(lane < (r + 1) * SSD_HEADDIM, vals[r], out)
    return out


def _ssd_kernel(x_ref, b_ref, c_ref, z_ref, wx_ref, wb_ref, wc_ref, bx_ref, bb_ref, bc_ref,
                gcol_ref, grow_ref, d_ref, gain_ref, out_ref):
    g = pl.program_id(1)
    s_len = x_ref.shape[1]
    L = SSD_CHUNK
    R = SSD_HEADS_PER_GROUP
    ri = lax.broadcasted_iota(jnp.int32, (L, L), 0)
    ci = lax.broadcasted_iota(jnp.int32, (L, L), 1)
    causal = ci <= ri
    lane_head = lax.broadcasted_iota(jnp.int32, (L, SSD_GROUP_W), 1) // SSD_HEADDIM

    state = jnp.zeros((SSD_STATE, SSD_GROUP_W), F32)
    cs_prev_end = None

    for c in range(s_len // L):
        rows = pl.ds(c * L, L)
        xs = _conv_silu(x_ref, c, wx_ref, bx_ref)
        bm = _conv_silu(b_ref, c, wb_ref, bb_ref)
        cm = _conv_silu(c_ref, c, wc_ref, bc_ref)
        bmb = bm.astype(BF16)
        cmb = cm.astype(BF16)
        gc = gcol_ref[0, rows, :]
        dt_cols = [_lane_column(gc, GATE_DT + R * g + r) for r in range(R)]
        cs_cols = [_lane_column(gc, GATE_CS + R * g + r) for r in range(R)]
        base, half = (c * L) // CHUNK * CHUNK, (c * L) % CHUNK
        cs_rows = [grow_ref[0, pl.ds(GATE_CS + R * g + r, 1), pl.ds(base, CHUNK)][:, half:half + L]
                   for r in range(R)]
        cs_raw_end = [col[L - 1:L, :] for col in cs_cols]
        if half:
            cs_cols = [col - off for col, off in zip(cs_cols, cs_prev_end)]
            cs_rows = [row - off for row, off in zip(cs_rows, cs_prev_end)]
        cs_prev_end = cs_raw_end

        dt_full = _per_head_lanes(dt_cols, (L, SSD_GROUP_W))
        cs_full = _per_head_lanes(cs_cols, (L, SSD_GROUP_W))
        cs_end = _per_head_lanes([col[L - 1:L, :] for col in cs_cols], (1, SSD_GROUP_W))

        xd = xs * dt_full
        xdb = xd.astype(BF16)
        cb = _dot_nt(cmb, bmb)
        y = _dot(cmb, state.astype(BF16)) * jnp.exp2(cs_full)
        for r in range(R):
            lmat = jnp.where(causal, jnp.exp2(cs_cols[r] - cs_rows[r]), 0.0)
            xr = jnp.where(lane_head == r, xdb, jnp.zeros_like(xdb))
            y = y + _dot((cb * lmat).astype(BF16), xr)
        y = y + d_ref[...] * xs
        y = y * _silu(z_ref[0, rows, :])
        out_ref[0, rows, :] = _rms(y, gain_ref[...]).astype(out_ref.dtype)

        decay_end = jnp.exp2(cs_end - cs_full)
        state = jnp.exp2(cs_end) * state + _dot(bm.T.astype(BF16), (xd * decay_end).astype(BF16))


def ssd(proj, conv_w, conv_b, gcol, grow, d_row, gain_row):
    b, s, _ = proj.shape
    G = SSD_GROUPS
    wide = lambda off: pl.BlockSpec((1, s, SSD_GROUP_W), lambda i, g: (i, 0, off // SSD_GROUP_W + g))
    narrow = lambda off: pl.BlockSpec((1, s, SSD_STATE), lambda i, g: (i, 0, off // SSD_STATE + g))
    cw = lambda rows, width, off: pl.BlockSpec((rows, width), lambda i, g: (0, off // width + g))
    return pl.pallas_call(
        _ssd_kernel,
        grid=(b, G),
        in_specs=[wide(OFF_X), narrow(OFF_B), narrow(OFF_C), wide(OFF_Z),
                  cw(SSD_CONV, SSD_GROUP_W, CONV_OFF_X), cw(SSD_CONV, SSD_STATE, CONV_OFF_B),
                  cw(SSD_CONV, SSD_STATE, CONV_OFF_C),
                  cw(1, SSD_GROUP_W, CONV_OFF_X), cw(1, SSD_STATE, CONV_OFF_B), cw(1, SSD_STATE, CONV_OFF_C),
                  pl.BlockSpec((1, s, GATE_W), lambda i, g: (i, 0, 0)),
                  pl.BlockSpec((1, GATE_W, s), lambda i, g: (i, 0, 0)),
                  cw(1, SSD_GROUP_W, 0), cw(1, SSD_GROUP_W, 0)],
        out_specs=pl.BlockSpec((1, s, SSD_GROUP_W), lambda i, g: (i, 0, g)),
        out_shape=jax.ShapeDtypeStruct((b, s, SSD_D_INNER), BF16),
        compiler_params=_cparams(("parallel", "parallel")),
        name="ssd",
    )(proj, proj, proj, proj, conv_w, conv_w, conv_w, conv_b, conv_b, conv_b, gcol, grow, d_row, gain_row)


def _moba_kernel(q_ref, k_ref, v_ref, cos_ref, sin_ref, side_ref, out_ref, side_out_ref,
                 qs_ref, qb_ref, kb_ref, vt_ref, s_ref, pb_ref):
    side_out_ref[...] = side_ref[...].astype(BF16)
    s_len = q_ref.shape[1]
    BLK = MOBA_BLOCK
    nb = s_len // BLK
    nbp = -(-nb // 8) * 8
    half = ATTN_HEAD_DIM // 2
    scale = ATTN_HEAD_DIM ** -0.5
    cos = cos_ref[...]
    sin = sin_ref[...]

    def rope(x):
        return x * cos + pltpu.roll(x, half, axis=1) * sin

    q = rope(q_ref[0])
    k = rope(k_ref[0])
    qs_ref[...] = q
    qb_ref[...] = (q * (scale * LOG2E)).astype(BF16)
    kb_ref[...] = k.astype(BF16)
    vt_ref[...] = v_ref[0].T.astype(BF16)

    row8 = lax.broadcasted_iota(jnp.int32, (nbp, ATTN_HEAD_DIM), 0)
    k_mean = jnp.zeros((nbp, ATTN_HEAD_DIM), F32)
    for j in range(nb):
        k_mean = jnp.where(row8 == j, jnp.mean(k[j * BLK:(j + 1) * BLK, :], axis=0, keepdims=True), k_mean)

    SUB = SUBLANES
    GRP = BLK // SUB
    gi = lax.broadcasted_iota(jnp.int32, (GRP, SUB, BLK), 0)
    si = lax.broadcasted_iota(jnp.int32, (GRP, SUB, BLK), 1)
    ci = lax.broadcasted_iota(jnp.int32, (GRP, SUB, BLK), 2)
    causal_bias = jnp.where(gi * SUB + si <= ci, 0.0, -jnp.inf)
    blk_row = lax.broadcasted_iota(jnp.int32, (nbp, BLK), 0)

    def scores(i):
        qrows = pl.ds(i * BLK, BLK)
        keep = None
        if i > MOBA_TOPK:
            gate = lax.dot_general(k_mean, qs_ref[qrows, :], (((1,), (1,)), ((), ())),
                                   precision=lax.Precision.HIGHEST, preferred_element_type=F32)
            rank = jnp.zeros((nbp, BLK), F32)
            for m in range(i):
                gm = gate[m:m + 1, :]
                ahead = (gm > gate) | ((gm == gate) & (blk_row > m))
                rank = rank + jnp.where(ahead, 1.0, 0.0)
            keep = rank < float(MOBA_TOPK)
        qb = qb_ref[qrows, :]
        m8 = None
        for j in range(i + 1):
            krows = pl.ds(j * BLK, BLK)
            sj = _dot_nt(kb_ref[krows, :], qb).reshape(GRP, SUB, BLK)
            if j == i:
                sj = sj + causal_bias
            elif keep is not None:
                sj = sj + jnp.broadcast_to(jnp.where(keep[j:j + 1, :], 0.0, -jnp.inf), (SUB, BLK))[None]
            s_ref[i % 2, krows, :] = sj.reshape(BLK, BLK)
            mj = jnp.max(sj, axis=0)
            m8 = mj if m8 is None else jnp.maximum(m8, mj)
        return m8

    def attend(i, m8):
        qrows = pl.ds(i * BLK, BLK)
        nk = (i + 1) * BLK
        m_row = jnp.broadcast_to(jnp.max(m8, axis=0, keepdims=True), (SUB, BLK))
        den8 = jnp.zeros((SUB, BLK), F32)
        for j in range(i + 1):
            krows = pl.ds(j * BLK, BLK)
            p = jnp.exp2(s_ref[i % 2, krows, :].reshape(GRP, SUB, BLK) - m_row[None])
            den8 = den8 + jnp.sum(p, axis=0)
            pb_ref[i % 2, krows, :] = p.reshape(BLK, BLK).astype(BF16)
        den = jnp.sum(den8, axis=0, keepdims=True)
        ot = _dot(vt_ref[:, pl.ds(0, nk)], pb_ref[i % 2, pl.ds(0, nk), :])
        out_ref[0, qrows, :] = (ot / den).T.astype(out_ref.dtype)

    m8_next = scores(0)
    for i in range(nb):
        m8_cur = m8_next
        if i + 1 < nb:
            m8_next = scores(i + 1)
        attend(i, m8_cur)


def moba(qkv, cos_full, sin_signed, w_side):
    b, s, _ = qkv.shape
    hd = ATTN_HEAD_DIM
    grid = (b, ATTN_HEADS)
    side_in, side_out, side_shape = _side_cast_specs(w_side, grid)
    blk = lambda part: pl.BlockSpec((1, s, hd), lambda i, h: (i, 0, part * ATTN_HEADS + h))
    return pl.pallas_call(
        _moba_kernel,
        grid=grid,
        in_specs=[blk(0), blk(1), blk(2),
                  pl.BlockSpec((s, hd), lambda i, h: (0, 0)),
                  pl.BlockSpec((s, hd), lambda i, h: (0, 0)),
                  side_in],
        out_specs=[pl.BlockSpec((1, s, hd), lambda i, h: (i, 0, h)), side_out],
        out_shape=[jax.ShapeDtypeStruct((b, s, ATTN_HEADS * hd), BF16), side_shape],
        scratch_shapes=[pltpu.VMEM((s, hd), F32), pltpu.VMEM((s, hd), BF16), pltpu.VMEM((s, hd), BF16),
                        pltpu.VMEM((hd, s), BF16), pltpu.VMEM((2, s, MOBA_BLOCK), F32),
                        pltpu.VMEM((2, s, MOBA_BLOCK), BF16)],
        compiler_params=_cparams(("parallel", "parallel")),
        name="moba",
    )(qkv, qkv, qkv, cos_full, sin_signed, w_side)


def _rope_tables(s):
    half = ATTN_HEAD_DIM // 2
    inv = ROPE_THETA ** (-jnp.arange(half, dtype=F32) / half)
    ang = jnp.arange(s).astype(F32)[:, None] * inv[None, :]
    cos, sin = jnp.cos(ang), jnp.sin(ang)
    return jnp.concatenate([cos, cos], axis=-1), jnp.concatenate([-sin, sin], axis=-1)


def kernel(x, l0_norm_mix, l0_w_in, l0_mlstm_gate_bias, l0_mlstm_norm, l0_ssd_conv_w, l0_ssd_conv_b,
           l0_ssd_dt_bias, l0_ssd_a_log, l0_ssd_d, l0_ssd_norm, l0_w_out, l0_norm_ffn, l0_ffn_gate,
           l0_ffn_up, l0_ffn_down, l1_norm_mix, l1_w_qkv, l1_w_o, l1_norm_ffn, l1_ffn_gate, l1_ffn_up,
           l1_ffn_down, final_norm):
    b, s, d = x.shape
    m = b * s
    assert d == D_MODEL and s % CHUNK == 0 and s % MOBA_BLOCK == 0 and m % PROJ_TM == 0
    xr = x.reshape(m, d)

    w_in_t = l0_w_in.T
    w_gate_t = jnp.concatenate([w_in_t[IN_QKVO_END:IN_ZX_START], w_in_t[IN_ZX_END:IN_END],
                                jnp.zeros((GATE_W - GATE_COLS, d), F32)], axis=0)
    pad = jnp.zeros((GATE_W - GATE_COLS,), F32)
    bias_row = jnp.concatenate([l0_mlstm_gate_bias, l0_ssd_dt_bias, pad]).reshape(1, GATE_W)
    alog_row = jnp.concatenate([jnp.zeros((GATE_DT,), F32), l0_ssd_a_log, pad]).reshape(1, GATE_W)
    xn, gcol, grow = norm_gate_prep(x, l0_norm_mix, w_gate_t, bias_row, alog_row)
    xn = xn.reshape(m, d)
    proj_a = wres_matmul(xn, w_in_t, 0, IN_QKVO_END, w_transposed=True, tm=PROJ_TM, tn=PROJ_TN, out_dtype=F32,
                         name="in_proj_qkvo").reshape(b, s, -1)
    proj_b = wres_matmul(xn, w_in_t, IN_ZX_START, IN_ZX_END - IN_ZX_START, w_transposed=True, tm=PROJ_TM,
                         tn=PROJ_TN, out_dtype=F32, name="in_proj_zx").reshape(b, s, -1)

    hm, w_out_bf16 = mlstm(proj_a, gcol, grow, l0_mlstm_norm, l0_w_out)
    hm = hm.reshape(m, D_MODEL)
    d_row = jnp.repeat(l0_ssd_d, SSD_HEADDIM).reshape(1, SSD_D_INNER)
    ys = ssd(proj_b, l0_ssd_conv_w, l0_ssd_conv_b.reshape(1, -1), gcol, grow, d_row,
             l0_ssd_norm.reshape(1, SSD_D_INNER)).reshape(m, SSD_D_INNER)
    xr, xn = matmul_residual_norm([hm, ys], w_out_bf16, xr, l0_norm_ffn, tm=OUT_PROJ_TM, emit_x=True,
                                  norm_dtype=BF16, name="l0_out_proj")
    hff, w_down_bf16 = swiglu_up(xn, l0_ffn_gate, l0_ffn_up, l0_ffn_down, tm=FFN_UP_TM, tn=FFN_UP_TN,
                                 name="l0_ffn_up")
    xr, xn = matmul_residual_norm([hff], w_down_bf16, xr, l1_norm_mix, tm=FFN_DOWN_TM, emit_x=True,
                                  norm_dtype=BF16, name="l0_ffn_down")

    qkv = wres_matmul(xn, l1_w_qkv, 0, 3 * D_MODEL, w_transposed=False, tm=PROJ_TM, tn=PROJ_TN, out_dtype=F32,
                      name="qkv_proj").reshape(b, s, 3 * D_MODEL)
    cos_full, sin_signed = _rope_tables(s)
    att, w_o_bf16 = moba(qkv, cos_full, sin_signed, l1_w_o)
    xr, xn = matmul_residual_norm([att.reshape(m, D_MODEL)], w_o_bf16, xr, l1_norm_ffn, tm=OUT_PROJ_TM,
                                  emit_x=True, norm_dtype=BF16, name="l1_out_proj")
    hff, w_down_bf16 = swiglu_up(xn, l1_ffn_gate, l1_ffn_up, l1_ffn_down, tm=FFN_UP_TM, tn=FFN_UP_TN,
                                 name="l1_ffn_up")
    (out,) = matmul_residual_norm([hff], w_down_bf16, xr, final_norm, tm=FFN_DOWN_TM, emit_x=False,
                                  norm_dtype=F32, name="l1_ffn_down")
    return out.reshape(b, s, d)
```

```python
import functools

import jax
import jax.numpy as jnp
from jax import lax
from jax.experimental import pallas as pl
from jax.experimental.pallas import tpu as pltpu

F32 = jnp.float32
BF16 = jnp.bfloat16

D_MODEL = 2048
NORM_EPS = 1e-6
MLSTM_HEADS = 8
MLSTM_DQK = 128
MLSTM_DV = 256
MLSTM_HEADS_PER_STEP = 2
SSD_D_INNER = 2048
SSD_HEADDIM = 64
SSD_HEADS = 32
SSD_STATE = 128
SSD_GROUPS = 8
SSD_HEADS_PER_GROUP = SSD_HEADS // SSD_GROUPS
SSD_GROUP_W = SSD_HEADS_PER_GROUP * SSD_HEADDIM
SSD_CONV = 4
ATTN_HEADS = 16
ATTN_HEAD_DIM = 128
MOBA_BLOCK = 256
MOBA_TOPK = 3
MOBA_HEADS_PER_STEP = 2
ROPE_THETA = 10000.0
LOG2E = 1.4426950408889634

MLSTM_QK_W = MLSTM_HEADS * MLSTM_DQK
MLSTM_V_W = MLSTM_HEADS * MLSTM_DV
SSD_BC_W = SSD_GROUPS * SSD_STATE
IN_QKVO_END = 2 * MLSTM_QK_W + 2 * MLSTM_V_W
IN_ZX_START = IN_QKVO_END + 2 * MLSTM_HEADS
IN_ZX_END = IN_ZX_START + 2 * SSD_D_INNER + 2 * SSD_BC_W
IN_END = IN_ZX_END + SSD_HEADS
OFF_Q, OFF_K, OFF_V, OFF_O = 0, MLSTM_QK_W, 2 * MLSTM_QK_W, 2 * MLSTM_QK_W + MLSTM_V_W
OFF_Z, OFF_X, OFF_B, OFF_C = 0, SSD_D_INNER, 2 * SSD_D_INNER, 2 * SSD_D_INNER + SSD_BC_W
CONV_OFF_X, CONV_OFF_B, CONV_OFF_C = 0, SSD_D_INNER, SSD_D_INNER + SSD_BC_W
GATE_I, GATE_F, GATE_DT, GATE_CS = 0, 8, 16, 48
GATE_COLS = 48
GATE_W = 128

CHUNK = 256
SSD_CHUNK = 128
NORM_GATE_ROWS = 1024
SUBLANES = 8
CONV_HALO = SUBLANES
BF16_TILE_ROWS = 16
TAIL_ROWS = 128
WCAST_ROWS = 256

V7X_VMEM_LIMIT_BYTES = 56 * 1024 * 1024

PROJ_TM, PROJ_TN = 1024, 1024
FFN_UP_TM, FFN_UP_TN = 1024, 512
OUT_PROJ_TM = 512
FFN_DOWN_TM = 256


def _cparams(sem):
    return pltpu.CompilerParams(dimension_semantics=sem, vmem_limit_bytes=V7X_VMEM_LIMIT_BYTES)


def _sigmoid(x):
    return 0.5 * jnp.tanh(0.5 * x) + 0.5


def _silu(x):
    h = 0.5 * x
    return h * jnp.tanh(h) + h


def _softplus(x):
    return jnp.maximum(x, 0.0) + jnp.log(1.0 + jnp.exp(-jnp.abs(x)))


def _rms(x, g):
    ms = jnp.mean(x * x, axis=-1, keepdims=True)
    return x * lax.rsqrt(ms + NORM_EPS) * g


def _dot(a, b):
    return jnp.dot(a, b, preferred_element_type=F32)


def _dot_nt(a, b):
    return lax.dot_general(a, b, (((1,), (1,)), ((), ())), preferred_element_type=F32)


def _side_cast_specs(w, grid):
    n_steps = grid[0] * grid[1]
    rows, cols = w.shape
    assert rows % n_steps == 0 and (rows // n_steps) % BF16_TILE_ROWS == 0
    blk = pl.BlockSpec((rows // n_steps, cols), lambda a, b_: (a * grid[1] + b_, 0))
    return blk, blk, jax.ShapeDtypeStruct((rows, cols), BF16)


def _swiglu_up_kernel(x_ref, wg_ref, wu_ref, side_ref, o_ref, side_out_ref, wg_scr, wu_scr):
    side_out_ref[...] = side_ref[...].astype(BF16)

    @pl.when(pl.program_id(1) == 0)
    def _():
        for r in range(0, wg_scr.shape[0], WCAST_ROWS):
            rows = pl.ds(r, WCAST_ROWS)
            wg_scr[rows, :] = wg_ref[rows, :].astype(BF16)
            wu_scr[rows, :] = wu_ref[rows, :].astype(BF16)

    x = x_ref[...]
    gate = _dot(x, wg_scr[...])
    up = _dot(x, wu_scr[...])
    o_ref[...] = (_silu(gate) * up).astype(o_ref.dtype)


def swiglu_up(xn, wg, wu, w_side, *, tm, tn, name):
    m, k = xn.shape
    n = wg.shape[1]
    grid = (n // tn, m // tm)
    side_in, side_out, side_shape = _side_cast_specs(w_side, grid)
    return pl.pallas_call(
        _swiglu_up_kernel,
        grid=grid,
        in_specs=[pl.BlockSpec((tm, k), lambda j, i: (i, 0)),
                  pl.BlockSpec((k, tn), lambda j, i: (0, j)),
                  pl.BlockSpec((k, tn), lambda j, i: (0, j)),
                  side_in],
        out_specs=[pl.BlockSpec((tm, tn), lambda j, i: (i, j)), side_out],
        out_shape=[jax.ShapeDtypeStruct((m, n), BF16), side_shape],
        scratch_shapes=[pltpu.VMEM((k, tn), BF16), pltpu.VMEM((k, tn), BF16)],
        compiler_params=_cparams(("parallel", "arbitrary")),
        name=name,
    )(xn, wg, wu, w_side)


def _matmul_residual_norm_kernel(*refs, n_a, emit_x):
    a_refs = refs[:n_a]
    w_ref, r_ref, g_ref = refs[n_a:n_a + 3]
    outs = refs[n_a + 3:]
    acc = r_ref[...]
    off = 0
    for a_ref in a_refs:
        kk = a_ref.shape[1]
        acc = acc + _dot(a_ref[...], w_ref[pl.ds(off, kk), :])
        off += kk
    if emit_x:
        outs[0][...] = acc
    xn_ref = outs[-1]
    xn_ref[...] = _rms(acc, g_ref[...]).astype(xn_ref.dtype)


def matmul_residual_norm(a_list, w, res, g, *, tm, emit_x, norm_dtype, name):
    m, n = res.shape
    k = w.shape[0]
    assert sum(a.shape[1] for a in a_list) == k
    row_blk = lambda width: pl.BlockSpec((tm, width), lambda i: (i, 0))
    out_specs = [row_blk(n)]
    out_shape = [jax.ShapeDtypeStruct((m, n), norm_dtype)]
    if emit_x:
        out_specs = [row_blk(n)] + out_specs
        out_shape = [jax.ShapeDtypeStruct((m, n), F32)] + out_shape
    return pl.pallas_call(
        functools.partial(_matmul_residual_norm_kernel, n_a=len(a_list), emit_x=emit_x),
        grid=(m // tm,),
        in_specs=[row_blk(a.shape[1]) for a in a_list] + [
            pl.BlockSpec((k, n), lambda i: (0, 0), pipeline_mode=pl.Buffered(1)),
            row_blk(n),
            pl.BlockSpec((1, n), lambda i: (0, 0))],
        out_specs=out_specs,
        out_shape=out_shape,
        compiler_params=_cparams(("parallel",)),
        name=name,
    )(*a_list, w, res, g.reshape(1, n))


def _wres_matmul_kernel(x_ref, *refs, shift, w_transposed):
    if shift == 0:
        wa_ref, o_ref, w_scr = refs
    else:
        wa_ref, wb_ref, o_ref, w_scr = refs

    @pl.when(pl.program_id(1) == 0)
    def _():
        if shift == 0:
            w_scr[...] = wa_ref[...].astype(BF16)
        else:
            tn = w_scr.shape[0]
            w_scr[pl.ds(0, tn - shift), :] = wa_ref[pl.ds(shift, tn - shift), :].astype(BF16)
            w_scr[pl.ds(tn - shift, shift), :] = wb_ref[pl.ds(0, shift), :].astype(BF16)

    mm = _dot_nt if w_transposed else _dot
    o_ref[...] = mm(x_ref[...], w_scr[...]).astype(o_ref.dtype)


def wres_matmul(x, w, start, n_out, *, w_transposed, tm, tn, out_dtype, name):
    m, k = x.shape
    shift = start % tn
    base = start - shift
    assert n_out % tn == 0 and (shift == 0 or (w_transposed and shift % BF16_TILE_ROWS == 0 and shift <= TAIL_ROWS))
    assert start + n_out <= w.shape[0 if w_transposed else 1]
    if w_transposed:
        w_specs = [pl.BlockSpec((tn, k), lambda j, i: (base // tn + j, 0))]
        if shift:
            w_specs.append(pl.BlockSpec((TAIL_ROWS, k), lambda j, i: ((base + (j + 1) * tn) // TAIL_ROWS, 0)))
        scratch = pltpu.VMEM((tn, k), BF16)
    else:
        w_specs = [pl.BlockSpec((k, tn), lambda j, i: (0, base // tn + j))]
        scratch = pltpu.VMEM((k, tn), BF16)
    return pl.pallas_call(
        functools.partial(_wres_matmul_kernel, shift=shift, w_transposed=w_transposed),
        grid=(n_out // tn, m // tm),
        in_specs=[pl.BlockSpec((tm, k), lambda j, i: (i, 0))] + w_specs,
        out_specs=pl.BlockSpec((tm, tn), lambda j, i: (i, j)),
        out_shape=jax.ShapeDtypeStruct((m, n_out), out_dtype),
        scratch_shapes=[scratch],
        compiler_params=_cparams(("parallel", "arbitrary")),
        name=name,
    )(x, *([w] * len(w_specs)))


def _cumsum_rows(x):
    n = x.shape[0]
    row = lax.broadcasted_iota(jnp.int32, x.shape, 0)
    shift = 1
    while shift < n:
        x = x + jnp.where(row >= shift, pltpu.roll(x, shift, axis=0), 0.0)
        shift *= 2
    return x


def _norm_gate_prep_kernel(x_ref, g_ref, w_ref, bias_ref, alog_ref, xn_ref, col_ref, row_ref):
    w = w_ref[...].astype(BF16)
    neg_a = -jnp.exp(alog_ref[...])
    for c in range(x_ref.shape[1] // CHUNK):
        rows = pl.ds(c * CHUNK, CHUNK)
        xn = _rms(x_ref[0, rows, :], g_ref[...]).astype(BF16)
        xn_ref[0, rows, :] = xn
        v = _dot_nt(xn, w) + bias_ref[...]
        lane = lax.broadcasted_iota(jnp.int32, v.shape, 1)
        is_f = (lane >= GATE_F) & (lane < GATE_DT)
        is_dt = (lane >= GATE_DT) & (lane < GATE_CS)
        logf = -_softplus(-v)
        dt = _softplus(v)
        da = jnp.where(is_dt, dt * neg_a, 0.0)
        pre = jnp.where(is_f, logf, 0.0) + pltpu.roll(da, GATE_CS - GATE_DT, axis=1)
        cum = _cumsum_rows(pre) * LOG2E
        out = jnp.where(lane < GATE_F, v * LOG2E, jnp.where(is_f, cum, jnp.where(is_dt, dt, cum)))
        col_ref[0, rows, :] = out
        row_ref[0, :, rows] = out.T


def norm_gate_prep(x, g, w_gate_t, bias_row, alog_row):
    b, s, k = x.shape
    w = w_gate_t.shape[0]
    tr = NORM_GATE_ROWS
    return pl.pallas_call(
        _norm_gate_prep_kernel,
        grid=(b, s // tr),
        in_specs=[pl.BlockSpec((1, tr, k), lambda i, c: (i, c, 0)),
                  pl.BlockSpec((1, k), lambda i, c: (0, 0)),
                  pl.BlockSpec((w, k), lambda i, c: (0, 0)),
                  pl.BlockSpec((1, w), lambda i, c: (0, 0)),
                  pl.BlockSpec((1, w), lambda i, c: (0, 0))],
        out_specs=[pl.BlockSpec((1, tr, k), lambda i, c: (i, c, 0)),
                   pl.BlockSpec((1, tr, w), lambda i, c: (i, c, 0)),
                   pl.BlockSpec((1, w, tr), lambda i, c: (i, 0, c))],
        out_shape=[jax.ShapeDtypeStruct((b, s, k), BF16),
                   jax.ShapeDtypeStruct((b, s, w), F32),
                   jax.ShapeDtypeStruct((b, w, s), F32)],
        compiler_params=_cparams(("parallel", "parallel")),
        name="norm_gate_prep",
    )(x, g.reshape(1, k), w_gate_t, bias_row, alog_row)


def _lane_column(tile, idx):
    lane = lax.broadcasted_iota(jnp.int32, tile.shape, 1)
    return jnp.sum(jnp.where(lane == idx, tile, 0.0), axis=1, keepdims=True)


def _mlstm_kernel(q_ref, k_ref, v_ref, o_ref, gcol_ref, grow_ref, gain_ref, side_ref, out_ref, side_out_ref):
    side_out_ref[...] = side_ref[...].astype(BF16)
    for hh in range(MLSTM_HEADS_PER_STEP):
        qk = pl.ds(hh * MLSTM_DQK, MLSTM_DQK)
        vo = pl.ds(hh * MLSTM_DV, MLSTM_DV)
        _mlstm_head(pl.program_id(1) * MLSTM_HEADS_PER_STEP + hh, q_ref.at[0, :, qk], k_ref.at[0, :, qk],
                    v_ref.at[0, :, vo], o_ref.at[0, :, vo], gcol_ref, grow_ref, gain_ref.at[hh], out_ref.at[0, :, vo])


def _mlstm_head(h, q_ref, k_ref, v_ref, o_ref, gcol_ref, grow_ref, gain_ref, out_ref):
    s_len = q_ref.shape[0]
    L = CHUNK
    scale = MLSTM_DQK ** -0.5
    ri = lax.broadcasted_iota(jnp.int32, (L, L), 0)
    ci = lax.broadcasted_iota(jnp.int32, (L, L), 1)
    causal = ci <= ri
    gain = gain_ref[...]

    def intra(c):
        rows = pl.ds(c * L, L)
        q = q_ref[rows, :] * scale
        k = k_ref[rows, :]
        qb = q.astype(BF16)
        kb = k.astype(BF16)
        vb = v_ref[rows, :].astype(BF16)
        gc = gcol_ref[0, rows, :]
        i_col = _lane_column(gc, GATE_I + h)
        b_col = _lane_column(gc, GATE_F + h)
        i_row = grow_ref[0, pl.ds(GATE_I + h, 1), rows]
        b_row = grow_ref[0, pl.ds(GATE_F + h, 1), rows]
        log_d = jnp.where(causal, b_col - (b_row - i_row), -jnp.inf)
        btot = b_col[L - 1:L, :]
        lw = btot - b_col + i_col
        return dict(q=q, k=k, qb=qb, vb=vb, b_col=b_col, log_d=log_d, qk=_dot_nt(qb, kb), btot=btot, lw=lw,
                    m_intra=jnp.max(log_d, axis=1, keepdims=True), lw_max=jnp.max(lw, axis=0, keepdims=True))

    def finish(c, t, carry):
        c_state, n_state, m_state = carry
        rows = pl.ds(c * L, L)
        log_inter = t["b_col"] + m_state
        m_row = jnp.maximum(log_inter, t["m_intra"])
        scores = t["qk"] * jnp.exp2(t["log_d"] - m_row)
        inter = jnp.exp2(log_inter - m_row)
        num = _dot(scores.astype(BF16), t["vb"]) + inter * _dot(t["qb"], c_state.astype(BF16))
        nq = jnp.sum(scores, axis=1, keepdims=True) + inter * jnp.sum(t["q"] * n_state, axis=1, keepdims=True)
        denom = jnp.maximum(jnp.abs(nq), jnp.exp2(-m_row))
        hm = num / denom
        y = _rms(hm, gain) * _sigmoid(o_ref[rows, :])
        out_ref[rows, :] = y.astype(out_ref.dtype)
        m_new = jnp.maximum(t["btot"] + m_state, t["lw_max"])
        decay = jnp.exp2(t["btot"] + m_state - m_new)
        wk = jnp.exp2(t["lw"] - m_new) * t["k"]
        c_state = decay * c_state + _dot(wk.T.astype(BF16), t["vb"])
        n_state = decay * n_state + jnp.sum(wk, axis=0, keepdims=True)
        return c_state, n_state, m_new

    carry = (jnp.zeros((MLSTM_DQK, MLSTM_DV), F32), jnp.zeros((1, MLSTM_DQK), F32), jnp.zeros((1, 1), F32))
    n_chunks = s_len // L
    nxt = intra(0)
    for c in range(n_chunks):
        cur = nxt
        if c + 1 < n_chunks:
            nxt = intra(c + 1)
        carry = finish(c, cur, carry)


def mlstm(proj, gcol, grow, gain, w_side):
    b, s, _ = proj.shape
    n = MLSTM_HEADS_PER_STEP
    grid = (b, MLSTM_HEADS // n)
    side_in, side_out, side_shape = _side_cast_specs(w_side, grid)
    qk_blk = lambda off: pl.BlockSpec((1, s, n * MLSTM_DQK), lambda i, h: (i, 0, off // (n * MLSTM_DQK) + h))
    v_blk = lambda off: pl.BlockSpec((1, s, n * MLSTM_DV), lambda i, h: (i, 0, off // (n * MLSTM_DV) + h))
    return pl.pallas_call(
        _mlstm_kernel,
        grid=grid,
        in_specs=[qk_blk(OFF_Q), qk_blk(OFF_K), v_blk(OFF_V), v_blk(OFF_O),
                  pl.BlockSpec((1, s, GATE_W), lambda i, h: (i, 0, 0)),
                  pl.BlockSpec((1, GATE_W, s), lambda i, h: (i, 0, 0)),
                  pl.BlockSpec((n, 1, MLSTM_DV), lambda i, h: (h, 0, 0)),
                  side_in],
        out_specs=[pl.BlockSpec((1, s, n * MLSTM_DV), lambda i, h: (i, 0, h)), side_out],
        out_shape=[jax.ShapeDtypeStruct((b, s, MLSTM_HEADS * MLSTM_DV), BF16), side_shape],
        compiler_params=_cparams(("parallel", "parallel")),
        name="mlstm",
    )(proj, proj, proj, proj, gcol, grow, gain.reshape(MLSTM_HEADS, 1, MLSTM_DV), w_side)


def _conv_silu(ref, c, w_ref, b_ref):
    L = SSD_CHUNK
    cur = ref[0, pl.ds(c * L, L), :]
    if c == 0:
        halo = jnp.zeros((CONV_HALO, cur.shape[1]), F32)
    else:
        halo = ref[0, pl.ds(c * L - CONV_HALO, CONV_HALO), :]
    ext = jnp.concatenate([halo, cur], axis=0)
    assert SSD_CONV == 4
    w = w_ref[...]
    prev = pltpu.roll(ext, 1, axis=0)
    u = w[1:2, :] * ext + w[0:1, :] * prev
    acc = (b_ref[...] + w[3:4, :] * cur + w[2:3, :] * prev[CONV_HALO:, :]) + pltpu.roll(u, 2, axis=0)[CONV_HALO:, :]
    return _silu(acc)


def _per_head_lanes(vals, shape):
    lane = lax.broadcasted_iota(jnp.int32, shape, 1)
    out = jnp.broadcast_to(vals[SSD_HEADS_PER_GROUP - 1], shape)
    for r in range(SSD_HEADS_PER_GROUP - 2, -1, -1):
        out = jnp.where(lane < (r + 1) * SSD_HEADDIM, vals[r], out)
    return out


def _ssd_kernel(x_ref, b_ref, c_ref, z_ref, wx_ref, wb_ref, wc_ref, bx_ref, bb_ref, bc_ref,
                gcol_ref, grow_ref, d_ref, gain_ref, out_ref):
    g = pl.program_id(1)
    s_len = x_ref.shape[1]
    L = SSD_CHUNK
    R = SSD_HEADS_PER_GROUP
    ri = lax.broadcasted_iota(jnp.int32, (L, L), 0)
    ci = lax.broadcasted_iota(jnp.int32, (L, L), 1)
    causal = ci <= ri
    lane_head = lax.broadcasted_iota(jnp.int32, (L, SSD_GROUP_W), 1) // SSD_HEADDIM

    state = jnp.zeros((SSD_STATE, SSD_GROUP_W), F32)
    cs_prev_end = None

    for c in range(s_len // L):
        rows = pl.ds(c * L, L)
        xs = _conv_silu(x_ref, c, wx_ref, bx_ref)
        bm = _conv_silu(b_ref, c, wb_ref, bb_ref)
        cm = _conv_silu(c_ref, c, wc_ref, bc_ref)
        bmb = bm.astype(BF16)
        cmb = cm.astype(BF16)
        gc = gcol_ref[0, rows, :]
        dt_cols = [_lane_column(gc, GATE_DT + R * g + r) for r in range(R)]
        cs_cols = [_lane_column(gc, GATE_CS + R * g + r) for r in range(R)]
        base, half = (c * L) // CHUNK * CHUNK, (c * L) % CHUNK
        cs_rows = [grow_ref[0, pl.ds(GATE_CS + R * g + r, 1), pl.ds(base, CHUNK)][:, half:half + L]
                   for r in range(R)]
        cs_raw_end = [col[L - 1:L, :] for col in cs_cols]
        if half:
            cs_cols = [col - off for col, off in zip(cs_cols, cs_prev_end)]
            cs_rows = [row - off for row, off in zip(cs_rows, cs_prev_end)]
        cs_prev_end = cs_raw_end

        dt_full = _per_head_lanes(dt_cols, (L, SSD_GROUP_W))
        cs_full = _per_head_lanes(cs_cols, (L, SSD_GROUP_W))
        cs_end = _per_head_lanes([col[L - 1:L, :] for col in cs_cols], (1, SSD_GROUP_W))

        xd = xs * dt_full
        xdb = xd.astype(BF16)
        cb = _dot_nt(cmb, bmb)
        y = _dot(cmb, state.astype(BF16)) * jnp.exp2(cs_full)
        for r in range(R):
            lmat = jnp.where(causal, jnp.exp2(cs_cols[r] - cs_rows[r]), 0.0)
            xr = jnp.where(lane_head == r, xdb, jnp.zeros_like(xdb))
            y = y + _dot((cb * lmat).astype(BF16), xr)
        y = y + d_ref[...] * xs
        y = y * _silu(z_ref[0, rows, :])
        out_ref[0, rows, :] = _rms(y, gain_ref[...]).astype(out_ref.dtype)

        decay_end = jnp.exp2(cs_end - cs_full)
        state = jnp.exp2(cs_end) * state + _dot(bm.T.astype(BF16), (xd * decay_end).astype(BF16))


def ssd(proj, conv_w, conv_b, gcol, grow, d_row, gain_row):
    b, s, _ = proj.shape
    G = SSD_GROUPS
    wide = lambda off: pl.BlockSpec((1, s, SSD_GROUP_W), lambda i, g: (i, 0, off // SSD_GROUP_W + g))
    narrow = lambda off: pl.BlockSpec((1, s, SSD_STATE), lambda i, g: (i, 0, off // SSD_STATE + g))
    cw = lambda rows, width, off: pl.BlockSpec((rows, width), lambda i, g: (0, off // width + g))
    return pl.pallas_call(
        _ssd_kernel,
        grid=(b, G),
        in_specs=[wide(OFF_X), narrow(OFF_B), narrow(OFF_C), wide(OFF_Z),
                  cw(SSD_CONV, SSD_GROUP_W, CONV_OFF_X), cw(SSD_CONV, SSD_STATE, CONV_OFF_B),
                  cw(SSD_CONV, SSD_STATE, CONV_OFF_C),
                  cw(1, SSD_GROUP_W, CONV_OFF_X), cw(1, SSD_STATE, CONV_OFF_B), cw(1, SSD_STATE, CONV_OFF_C),
                  pl.BlockSpec((1, s, GATE_W), lambda i, g: (i, 0, 0)),
                  pl.BlockSpec((1, GATE_W, s), lambda i, g: (i, 0, 0)),
                  cw(1, SSD_GROUP_W, 0), cw(1, SSD_GROUP_W, 0)],
        out_specs=pl.BlockSpec((1, s, SSD_GROUP_W), lambda i, g: (i, 0, g)),
        out_shape=jax.ShapeDtypeStruct((b, s, SSD_D_INNER), BF16),
        compiler_params=_cparams(("parallel", "parallel")),
        name="ssd",
    )(proj, proj, proj, proj, conv_w, conv_w, conv_w, conv_b, conv_b, conv_b, gcol, grow, d_row, gain_row)


def _moba_kernel(q_ref, k_ref, v_ref, cos_ref, sin_ref, side_ref, out_ref, side_out_ref,
                 qs_ref, qb_ref, kb_ref, vt_ref, s_ref, pb_ref):
    side_out_ref[...] = side_ref[...].astype(BF16)
    for hh in range(MOBA_HEADS_PER_STEP):
        lanes = pl.ds(hh * ATTN_HEAD_DIM, ATTN_HEAD_DIM)
        _moba_head(q_ref.at[0, :, lanes], k_ref.at[0, :, lanes], v_ref.at[0, :, lanes], cos_ref, sin_ref,
                   out_ref.at[0, :, lanes], qs_ref, qb_ref, kb_ref, vt_ref, s_ref, pb_ref)


def _moba_head(q_ref, k_ref, v_ref, cos_ref, sin_ref, out_ref, qs_ref, qb_ref, kb_ref, vt_ref, s_ref, pb_ref):
    s_len = q_ref.shape[0]
    BLK = MOBA_BLOCK
    nb = s_len // BLK
    nbp = -(-nb // 8) * 8
    half = ATTN_HEAD_DIM // 2
    scale = ATTN_HEAD_DIM ** -0.5
    cos = cos_ref[...]
    sin = sin_ref[...]

    def rope(x):
        return x * cos + pltpu.roll(x, half, axis=1) * sin

    q = rope(q_ref[...])
    k = rope(k_ref[...])
    qs_ref[...] = q
    qb_ref[...] = (q * (scale * LOG2E)).astype(BF16)
    kb_ref[...] = k.astype(BF16)
    vt_ref[...] = v_ref[...].T.astype(BF16)

    row8 = lax.broadcasted_iota(jnp.int32, (nbp, ATTN_HEAD_DIM), 0)
    k_mean = jnp.zeros((nbp, ATTN_HEAD_DIM), F32)
    for j in range(nb):
        k_mean = jnp.where(row8 == j, jnp.mean(k[j * BLK:(j + 1) * BLK, :], axis=0, keepdims=True), k_mean)

    SUB = SUBLANES
    GRP = BLK // SUB
    gi = lax.broadcasted_iota(jnp.int32, (GRP, SUB, BLK), 0)
    si = lax.broadcasted_iota(jnp.int32, (GRP, SUB, BLK), 1)
    ci = lax.broadcasted_iota(jnp.int32, (GRP, SUB, BLK), 2)
    causal_bias = jnp.where(gi * SUB + si <= ci, 0.0, -jnp.inf)
    blk_row = lax.broadcasted_iota(jnp.int32, (nbp, BLK), 0)

    def scores(i):
        qrows = pl.ds(i * BLK, BLK)
        keep = None
        if i > MOBA_TOPK:
            gate = lax.dot_general(k_mean, qs_ref[qrows, :], (((1,), (1,)), ((), ())),
                                   precision=lax.Precision.HIGHEST, preferred_element_type=F32)
            rank = jnp.zeros((nbp, BLK), F32)
            for m in range(i):
                gm = gate[m:m + 1, :]
                ahead = (gm > gate) | ((gm == gate) & (blk_row > m))
                rank = rank + jnp.where(ahead, 1.0, 0.0)
            keep = rank < float(MOBA_TOPK)
        qb = qb_ref[qrows, :]
        m8 = None
        for j in range(i + 1):
            krows = pl.ds(j * BLK, BLK)
            sj = _dot_nt(kb_ref[krows, :], qb).reshape(GRP, SUB, BLK)
            if j == i:
                sj = sj + causal_bias
            elif keep is not None:
                sj = sj + jnp.broadcast_to(jnp.where(keep[j:j + 1, :], 0.0, -jnp.inf), (SUB, BLK))[None]
            s_ref[i % 2, krows, :] = sj.reshape(BLK, BLK)
            mj = jnp.max(sj, axis=0)
            m8 = mj if m8 is None else jnp.maximum(m8, mj)
        return m8

    def attend(i, m8):
        qrows = pl.ds(i * BLK, BLK)
        nk = (i + 1) * BLK
        m_row = jnp.broadcast_to(jnp.max(m8, axis=0, keepdims=True), (SUB, BLK))
        den8 = jnp.zeros((SUB, BLK), F32)
        for j in range(i + 1):
            krows = pl.ds(j * BLK, BLK)
            p = jnp.exp2(s_ref[i % 2, krows, :].reshape(GRP, SUB, BLK) - m_row[None])
            den8 = den8 + jnp.sum(p, axis=0)
            pb_ref[i % 2, krows, :] = p.reshape(BLK, BLK).astype(BF16)
        den = jnp.sum(den8, axis=0, keepdims=True)
        ot = _dot(vt_ref[:, pl.ds(0, nk)], pb_ref[i % 2, pl.ds(0, nk), :])
        out_ref[qrows, :] = (ot / den).T.astype(out_ref.dtype)

    m8_next = scores(0)
    for i in range(nb):
        m8_cur = m8_next
        if i + 1 < nb:
            m8_next = scores(i + 1)
        attend(i, m8_cur)


def moba(qkv, cos_full, sin_signed, w_side):
    b, s, _ = qkv.shape
    hd = ATTN_HEAD_DIM
    steps = ATTN_HEADS // MOBA_HEADS_PER_STEP
    grid = (b, steps)
    side_in, side_out, side_shape = _side_cast_specs(w_side, grid)
    blk = lambda part: pl.BlockSpec((1, s, MOBA_HEADS_PER_STEP * hd), lambda i, h: (i, 0, part * steps + h))
    return pl.pallas_call(
        _moba_kernel,
        grid=grid,
        in_specs=[blk(0), blk(1), blk(2),
                  pl.BlockSpec((s, hd), lambda i, h: (0, 0)),
                  pl.BlockSpec((s, hd), lambda i, h: (0, 0)),
                  side_in],
        out_specs=[pl.BlockSpec((1, s, MOBA_HEADS_PER_STEP * hd), lambda i, h: (i, 0, h)), side_out],
        out_shape=[jax.ShapeDtypeStruct((b, s, ATTN_HEADS * hd), BF16), side_shape],
        scratch_shapes=[pltpu.VMEM((s, hd), F32), pltpu.VMEM((s, hd), BF16), pltpu.VMEM((s, hd), BF16),
                        pltpu.VMEM((hd, s), BF16), pltpu.VMEM((2, s, MOBA_BLOCK), F32),
                        pltpu.VMEM((2, s, MOBA_BLOCK), BF16)],
        compiler_params=_cparams(("parallel", "parallel")),
        name="moba",
    )(qkv, qkv, qkv, cos_full, sin_signed, w_side)


def _rope_tables(s):
    half = ATTN_HEAD_DIM // 2
    inv = ROPE_THETA ** (-jnp.arange(half, dtype=F32) / half)
    ang = jnp.arange(s).astype(F32)[:, None] * inv[None, :]
    cos, sin = jnp.cos(ang), jnp.sin(ang)
    return jnp.concatenate([cos, cos], axis=-1), jnp.concatenate([-sin, sin], axis=-1)


def kernel(x, l0_norm_mix, l0_w_in, l0_mlstm_gate_bias, l0_mlstm_norm, l0_ssd_conv_w, l0_ssd_conv_b,
           l0_ssd_dt_bias, l0_ssd_a_log, l0_ssd_d, l0_ssd_norm, l0_w_out, l0_norm_ffn, l0_ffn_gate,
           l0_ffn_up, l0_ffn_down, l1_norm_mix, l1_w_qkv, l1_w_o, l1_norm_ffn, l1_ffn_gate, l1_ffn_up,
           l1_ffn_down, final_norm):
    b, s, d = x.shape
    m = b * s
    assert d == D_MODEL and s % CHUNK == 0 and s % MOBA_BLOCK == 0 and m % PROJ_TM == 0
    xr = x.reshape(m, d)

    w_in_t = l0_w_in.T
    w_gate_t = jnp.concatenate([w_in_t[IN_QKVO_END:IN_ZX_START], w_in_t[IN_ZX_END:IN_END],
                                jnp.zeros((GATE_W - GATE_COLS, d), F32)], axis=0)
    pad = jnp.zeros((GATE_W - GATE_COLS,), F32)
    bias_row = jnp.concatenate([l0_mlstm_gate_bias, l0_ssd_dt_bias, pad]).reshape(1, GATE_W)
    alog_row = jnp.concatenate([jnp.zeros((GATE_DT,), F32), l0_ssd_a_log, pad]).reshape(1, GATE_W)
    xn, gcol, grow = norm_gate_prep(x, l0_norm_mix, w_gate_t, bias_row, alog_row)
    xn = xn.reshape(m, d)
    proj_a = wres_matmul(xn, w_in_t, 0, IN_QKVO_END, w_transposed=True, tm=PROJ_TM, tn=PROJ_TN, out_dtype=F32,
                         name="in_proj_qkvo").reshape(b, s, -1)
    proj_b = wres_matmul(xn, w_in_t, IN_ZX_START, IN_ZX_END - IN_ZX_START, w_transposed=True, tm=PROJ_TM,
                         tn=PROJ_TN, out_dtype=F32, name="in_proj_zx").reshape(b, s, -1)

    hm, w_out_bf16 = mlstm(proj_a, gcol, grow, l0_mlstm_norm, l0_w_out)
    hm = hm.reshape(m, D_MODEL)
    d_row = jnp.repeat(l0_ssd_d, SSD_HEADDIM).reshape(1, SSD_D_INNER)
    ys = ssd(proj_b, l0_ssd_conv_w, l0_ssd_conv_b.reshape(1, -1), gcol, grow, d_row,
             l0_ssd_norm.reshape(1, SSD_D_INNER)).reshape(m, SSD_D_INNER)
    xr, xn = matmul_residual_norm([hm, ys], w_out_bf16, xr, l0_norm_ffn, tm=OUT_PROJ_TM, emit_x=True,
                                  norm_dtype=BF16, name="l0_out_proj")
    hff, w_down_bf16 = swiglu_up(xn, l0_ffn_gate, l0_ffn_up, l0_ffn_down, tm=FFN_UP_TM, tn=FFN_UP_TN,
                                 name="l0_ffn_up")
    xr, xn = matmul_residual_norm([hff], w_down_bf16, xr, l1_norm_mix, tm=FFN_DOWN_TM, emit_x=True,
                                  norm_dtype=BF16, name="l0_ffn_down")

    qkv = wres_matmul(xn, l1_w_qkv, 0, 3 * D_MODEL, w_transposed=False, tm=PROJ_TM, tn=PROJ_TN, out_dtype=F32,
                      name="qkv_proj").reshape(b, s, 3 * D_MODEL)
    cos_full, sin_signed = _rope_tables(s)
    att, w_o_bf16 = moba(qkv, cos_full, sin_signed, l1_w_o)
    xr, xn = matmul_residual_norm([att.reshape(m, D_MODEL)], w_o_bf16, xr, l1_norm_ffn, tm=OUT_PROJ_TM,
                                  emit_x=True, norm_dtype=BF16, name="l1_out_proj")
    hff, w_down_bf16 = swiglu_up(xn, l1_ffn_gate, l1_ffn_up, l1_ffn_down, tm=FFN_UP_TM, tn=FFN_UP_TN,
                                 name="l1_ffn_up")
    (out,) = matmul_residual_norm([hff], w_down_bf16, xr, final_norm, tm=FFN_DOWN_TM, emit_x=False,
                                  norm_dtype=F32, name="l1_ffn_down")
    return out.reshape(b, s, d)
```

```python
import functools

import jax
import jax.numpy as jnp
from jax import lax
from jax.experimental import pallas as pl
from jax.experimental.pallas import tpu as pltpu

F32 = jnp.float32
BF16 = jnp.bfloat16

D_MODEL = 2048
NORM_EPS = 1e-6
MLSTM_HEADS = 8
MLSTM_DQK = 128
MLSTM_DV = 256
MLSTM_HEADS_PER_STEP = 2
SSD_D_INNER = 2048
SSD_HEADDIM = 64
SSD_HEADS = 32
SSD_STATE = 128
SSD_GROUPS = 8
SSD_HEADS_PER_GROUP = SSD_HEADS // SSD_GROUPS
SSD_GROUP_W = SSD_HEADS_PER_GROUP * SSD_HEADDIM
SSD_CONV = 4
ATTN_HEADS = 16
ATTN_HEAD_DIM = 128
MOBA_BLOCK = 256
MOBA_TOPK = 3
MOBA_HEADS_PER_STEP = 4
ROPE_THETA = 10000.0
LOG2E = 1.4426950408889634

MLSTM_QK_W = MLSTM_HEADS * MLSTM_DQK
MLSTM_V_W = MLSTM_HEADS * MLSTM_DV
SSD_BC_W = SSD_GROUPS * SSD_STATE
IN_QKVO_END = 2 * MLSTM_QK_W + 2 * MLSTM_V_W
IN_ZX_START = IN_QKVO_END + 2 * MLSTM_HEADS
IN_ZX_END = IN_ZX_START + 2 * SSD_D_INNER + 2 * SSD_BC_W
IN_END = IN_ZX_END + SSD_HEADS
OFF_Q, OFF_K, OFF_V, OFF_O = 0, MLSTM_QK_W, 2 * MLSTM_QK_W, 2 * MLSTM_QK_W + MLSTM_V_W
OFF_Z, OFF_X, OFF_B, OFF_C = 0, SSD_D_INNER, 2 * SSD_D_INNER, 2 * SSD_D_INNER + SSD_BC_W
CONV_OFF_X, CONV_OFF_B, CONV_OFF_C = 0, SSD_D_INNER, SSD_D_INNER + SSD_BC_W
GATE_I, GATE_F, GATE_DT, GATE_CS = 0, 8, 16, 48
GATE_COLS = 48
GATE_W = 128

CHUNK = 256
SSD_CHUNK = 128
NORM_GATE_ROWS = 1024
SUBLANES = 8
CONV_HALO = SUBLANES
BF16_TILE_ROWS = 16
TAIL_ROWS = 128
WCAST_ROWS = 256

V7X_VMEM_LIMIT_BYTES = 56 * 1024 * 1024

PROJ_TM, PROJ_TN = 1024, 1024
FFN_UP_TM, FFN_UP_TN = 1024, 512
OUT_PROJ_TM = 512
FFN_DOWN_TM = 256


def _cparams(sem):
    return pltpu.CompilerParams(dimension_semantics=sem, vmem_limit_bytes=V7X_VMEM_LIMIT_BYTES)


def _sigmoid(x):
    return 0.5 * jnp.tanh(0.5 * x) + 0.5


def _silu(x):
    h = 0.5 * x
    return h * jnp.tanh(h) + h


def _softplus(x):
    return jnp.maximum(x, 0.0) + jnp.log(1.0 + jnp.exp(-jnp.abs(x)))


def _rms(x, g):
    ms = jnp.mean(x * x, axis=-1, keepdims=True)
    return x * lax.rsqrt(ms + NORM_EPS) * g


def _dot(a, b):
    return jnp.dot(a, b, preferred_element_type=F32)


def _dot_nt(a, b):
    return lax.dot_general(a, b, (((1,), (1,)), ((), ())), preferred_element_type=F32)


def _side_cast_specs(w, grid):
    n_steps = grid[0] * grid[1]
    rows, cols = w.shape
    assert rows % n_steps == 0 and (rows // n_steps) % BF16_TILE_ROWS == 0
    blk = pl.BlockSpec((rows // n_steps, cols), lambda a, b_: (a * grid[1] + b_, 0))
    return blk, blk, jax.ShapeDtypeStruct((rows, cols), BF16)


def _swiglu_up_kernel(x_ref, wg_ref, wu_ref, side_ref, o_ref, side_out_ref, wg_scr, wu_scr):
    side_out_ref[...] = side_ref[...].astype(BF16)

    @pl.when(pl.program_id(1) == 0)
    def _():
        for r in range(0, wg_scr.shape[0], WCAST_ROWS):
            rows = pl.ds(r, WCAST_ROWS)
            wg_scr[rows, :] = wg_ref[rows, :].astype(BF16)
            wu_scr[rows, :] = wu_ref[rows, :].astype(BF16)

    x = x_ref[...]
    gate = _dot(x, wg_scr[...])
    up = _dot(x, wu_scr[...])
    o_ref[...] = (_silu(gate) * up).astype(o_ref.dtype)


def swiglu_up(xn, wg, wu, w_side, *, tm, tn, name):
    m, k = xn.shape
    n = wg.shape[1]
    grid = (n // tn, m // tm)
    side_in, side_out, side_shape = _side_cast_specs(w_side, grid)
    return pl.pallas_call(
        _swiglu_up_kernel,
        grid=grid,
        in_specs=[pl.BlockSpec((tm, k), lambda j, i: (i, 0)),
                  pl.BlockSpec((k, tn), lambda j, i: (0, j)),
                  pl.BlockSpec((k, tn), lambda j, i: (0, j)),
                  side_in],
        out_specs=[pl.BlockSpec((tm, tn), lambda j, i: (i, j)), side_out],
        out_shape=[jax.ShapeDtypeStruct((m, n), BF16), side_shape],
        scratch_shapes=[pltpu.VMEM((k, tn), BF16), pltpu.VMEM((k, tn), BF16)],
        compiler_params=_cparams(("parallel", "arbitrary")),
        name=name,
    )(xn, wg, wu, w_side)


def _matmul_residual_norm_kernel(*refs, n_a, emit_x):
    a_refs = refs[:n_a]
    w_ref, r_ref, g_ref = refs[n_a:n_a + 3]
    outs = refs[n_a + 3:]
    acc = r_ref[...]
    off = 0
    for a_ref in a_refs:
        kk = a_ref.shape[1]
        acc = acc + _dot(a_ref[...], w_ref[pl.ds(off, kk), :])
        off += kk
    if emit_x:
        outs[0][...] = acc
    xn_ref = outs[-1]
    xn_ref[...] = _rms(acc, g_ref[...]).astype(xn_ref.dtype)


def matmul_residual_norm(a_list, w, res, g, *, tm, emit_x, norm_dtype, name):
    m, n = res.shape
    k = w.shape[0]
    assert sum(a.shape[1] for a in a_list) == k
    row_blk = lambda width: pl.BlockSpec((tm, width), lambda i: (i, 0))
    out_specs = [row_blk(n)]
    out_shape = [jax.ShapeDtypeStruct((m, n), norm_dtype)]
    if emit_x:
        out_specs = [row_blk(n)] + out_specs
        out_shape = [jax.ShapeDtypeStruct((m, n), F32)] + out_shape
    return pl.pallas_call(
        functools.partial(_matmul_residual_norm_kernel, n_a=len(a_list), emit_x=emit_x),
        grid=(m // tm,),
        in_specs=[row_blk(a.shape[1]) for a in a_list] + [
            pl.BlockSpec((k, n), lambda i: (0, 0), pipeline_mode=pl.Buffered(1)),
            row_blk(n),
            pl.BlockSpec((1, n), lambda i: (0, 0))],
        out_specs=out_specs,
        out_shape=out_shape,
        compiler_params=_cparams(("parallel",)),
        name=name,
    )(*a_list, w, res, g.reshape(1, n))


def _wres_matmul_kernel(x_ref, *refs, shift, w_transposed):
    if shift == 0:
        wa_ref, o_ref, w_scr = refs
    else:
        wa_ref, wb_ref, o_ref, w_scr = refs

    @pl.when(pl.program_id(1) == 0)
    def _():
        if shift == 0:
            w_scr[...] = wa_ref[...].astype(BF16)
        else:
            tn = w_scr.shape[0]
            w_scr[pl.ds(0, tn - shift), :] = wa_ref[pl.ds(shift, tn - shift), :].astype(BF16)
            w_scr[pl.ds(tn - shift, shift), :] = wb_ref[pl.ds(0, shift), :].astype(BF16)

    mm = _dot_nt if w_transposed else _dot
    o_ref[...] = mm(x_ref[...], w_scr[...]).astype(o_ref.dtype)


def wres_matmul(x, w, start, n_out, *, w_transposed, tm, tn, out_dtype, name):
    m, k = x.shape
    shift = start % tn
    base = start - shift
    assert n_out % tn == 0 and (shift == 0 or (w_transposed and shift % BF16_TILE_ROWS == 0 and shift <= TAIL_ROWS))
    assert start + n_out <= w.shape[0 if w_transposed else 1]
    if w_transposed:
        w_specs = [pl.BlockSpec((tn, k), lambda j, i: (base // tn + j, 0))]
        if shift:
            w_specs.append(pl.BlockSpec((TAIL_ROWS, k), lambda j, i: ((base + (j + 1) * tn) // TAIL_ROWS, 0)))
        scratch = pltpu.VMEM((tn, k), BF16)
    else:
        w_specs = [pl.BlockSpec((k, tn), lambda j, i: (0, base // tn + j))]
        scratch = pltpu.VMEM((k, tn), BF16)
    return pl.pallas_call(
        functools.partial(_wres_matmul_kernel, shift=shift, w_transposed=w_transposed),
        grid=(n_out // tn, m // tm),
        in_specs=[pl.BlockSpec((tm, k), lambda j, i: (i, 0))] + w_specs,
        out_specs=pl.BlockSpec((tm, tn), lambda j, i: (i, j)),
        out_shape=jax.ShapeDtypeStruct((m, n_out), out_dtype),
        scratch_shapes=[scratch],
        compiler_params=_cparams(("parallel", "arbitrary")),
        name=name,
    )(x, *([w] * len(w_specs)))


def _cumsum_rows(x):
    n = x.shape[0]
    row = lax.broadcasted_iota(jnp.int32, x.shape, 0)
    shift = 1
    while shift < n:
        x = x + jnp.where(row >= shift, pltpu.roll(x, shift, axis=0), 0.0)
        shift *= 2
    return x


def _norm_gate_prep_kernel(x_ref, g_ref, w_ref, bias_ref, alog_ref, xn_ref, col_ref, row_ref):
    w = w_ref[...].astype(BF16)
    neg_a = -jnp.exp(alog_ref[...])
    for c in range(x_ref.shape[1] // CHUNK):
        rows = pl.ds(c * CHUNK, CHUNK)
        xn = _rms(x_ref[0, rows, :], g_ref[...]).astype(BF16)
        xn_ref[0, rows, :] = xn
        v = _dot_nt(xn, w) + bias_ref[...]
        lane = lax.broadcasted_iota(jnp.int32, v.shape, 1)
        is_f = (lane >= GATE_F) & (lane < GATE_DT)
        is_dt = (lane >= GATE_DT) & (lane < GATE_CS)
        logf = -_softplus(-v)
        dt = _softplus(v)
        da = jnp.where(is_dt, dt * neg_a, 0.0)
        pre = jnp.where(is_f, logf, 0.0) + pltpu.roll(da, GATE_CS - GATE_DT, axis=1)
        cum = _cumsum_rows(pre) * LOG2E
        out = jnp.where(lane < GATE_F, v * LOG2E, jnp.where(is_f, cum, jnp.where(is_dt, dt, cum)))
        col_ref[0, rows, :] = out
        row_ref[0, :, rows] = out.T


def norm_gate_prep(x, g, w_gate_t, bias_row, alog_row):
    b, s, k = x.shape
    w = w_gate_t.shape[0]
    tr = NORM_GATE_ROWS
    return pl.pallas_call(
        _norm_gate_prep_kernel,
        grid=(b, s // tr),
        in_specs=[pl.BlockSpec((1, tr, k), lambda i, c: (i, c, 0)),
                  pl.BlockSpec((1, k), lambda i, c: (0, 0)),
                  pl.BlockSpec((w, k), lambda i, c: (0, 0)),
                  pl.BlockSpec((1, w), lambda i, c: (0, 0)),
                  pl.BlockSpec((1, w), lambda i, c: (0, 0))],
        out_specs=[pl.BlockSpec((1, tr, k), lambda i, c: (i, c, 0)),
                   pl.BlockSpec((1, tr, w), lambda i, c: (i, c, 0)),
                   pl.BlockSpec((1, w, tr), lambda i, c: (i, 0, c))],
        out_shape=[jax.ShapeDtypeStruct((b, s, k), BF16),
                   jax.ShapeDtypeStruct((b, s, w), F32),
                   jax.ShapeDtypeStruct((b, w, s), F32)],
        compiler_params=_cparams(("parallel", "parallel")),
        name="norm_gate_prep",
    )(x, g.reshape(1, k), w_gate_t, bias_row, alog_row)


def _lane_column(tile, idx):
    lane = lax.broadcasted_iota(jnp.int32, tile.shape, 1)
    return jnp.sum(jnp.where(lane == idx, tile, 0.0), axis=1, keepdims=True)


def _mlstm_kernel(q_ref, k_ref, v_ref, o_ref, gcol_ref, grow_ref, gain_ref, side_ref, out_ref, side_out_ref):
    side_out_ref[...] = side_ref[...].astype(BF16)
    for hh in range(MLSTM_HEADS_PER_STEP):
        qk = pl.ds(hh * MLSTM_DQK, MLSTM_DQK)
        vo = pl.ds(hh * MLSTM_DV, MLSTM_DV)
        _mlstm_head(pl.program_id(1) * MLSTM_HEADS_PER_STEP + hh, q_ref.at[0, :, qk], k_ref.at[0, :, qk],
                    v_ref.at[0, :, vo], o_ref.at[0, :, vo], gcol_ref, grow_ref, gain_ref.at[hh], out_ref.at[0, :, vo])


def _mlstm_head(h, q_ref, k_ref, v_ref, o_ref, gcol_ref, grow_ref, gain_ref, out_ref):
    s_len = q_ref.shape[0]
    L = CHUNK
    scale = MLSTM_DQK ** -0.5
    ri = lax.broadcasted_iota(jnp.int32, (L, L), 0)
    ci = lax.broadcasted_iota(jnp.int32, (L, L), 1)
    causal = ci <= ri
    gain = gain_ref[...]

    def intra(c):
        rows = pl.ds(c * L, L)
        q = q_ref[rows, :] * scale
        k = k_ref[rows, :]
        qb = q.astype(BF16)
        kb = k.astype(BF16)
        vb = v_ref[rows, :].astype(BF16)
        gc = gcol_ref[0, rows, :]
        i_col = _lane_column(gc, GATE_I + h)
        b_col = _lane_column(gc, GATE_F + h)
        i_row = grow_ref[0, pl.ds(GATE_I + h, 1), rows]
        b_row = grow_ref[0, pl.ds(GATE_F + h, 1), rows]
        log_d = jnp.where(causal, b_col - (b_row - i_row), -jnp.inf)
        btot = b_col[L - 1:L, :]
        lw = btot - b_col + i_col
        return dict(q=q, k=k, qb=qb, vb=vb, b_col=b_col, log_d=log_d, qk=_dot_nt(qb, kb), btot=btot, lw=lw,
                    m_intra=jnp.max(log_d, axis=1, keepdims=True), lw_max=jnp.max(lw, axis=0, keepdims=True))

    def finish(c, t, carry):
        c_state, n_state, m_state = carry
        rows = pl.ds(c * L, L)
        log_inter = t["b_col"] + m_state
        m_row = jnp.maximum(log_inter, t["m_intra"])
        scores = t["qk"] * jnp.exp2(t["log_d"] - m_row)
        inter = jnp.exp2(log_inter - m_row)
        num = _dot(scores.astype(BF16), t["vb"]) + inter * _dot(t["qb"], c_state.astype(BF16))
        nq = jnp.sum(scores, axis=1, keepdims=True) + inter * jnp.sum(t["q"] * n_state, axis=1, keepdims=True)
        denom = jnp.maximum(jnp.abs(nq), jnp.exp2(-m_row))
        hm = num / denom
        y = _rms(hm, gain) * _sigmoid(o_ref[rows, :])
        out_ref[rows, :] = y.astype(out_ref.dtype)
        m_new = jnp.maximum(t["btot"] + m_state, t["lw_max"])
        decay = jnp.exp2(t["btot"] + m_state - m_new)
        wk = jnp.exp2(t["lw"] - m_new) * t["k"]
        c_state = decay * c_state + _dot(wk.T.astype(BF16), t["vb"])
        n_state = decay * n_state + jnp.sum(wk, axis=0, keepdims=True)
        return c_state, n_state, m_new

    carry = (jnp.zeros((MLSTM_DQK, MLSTM_DV), F32), jnp.zeros((1, MLSTM_DQK), F32), jnp.zeros((1, 1), F32))
    n_chunks = s_len // L
    nxt = intra(0)
    for c in range(n_chunks):
        cur = nxt
        if c + 1 < n_chunks:
            nxt = intra(c + 1)
        carry = finish(c, cur, carry)


def mlstm(proj, gcol, grow, gain, w_side):
    b, s, _ = proj.shape
    n = MLSTM_HEADS_PER_STEP
    grid = (b, MLSTM_HEADS // n)
    side_in, side_out, side_shape = _side_cast_specs(w_side, grid)
    qk_blk = lambda off: pl.BlockSpec((1, s, n * MLSTM_DQK), lambda i, h: (i, 0, off // (n * MLSTM_DQK) + h))
    v_blk = lambda off: pl.BlockSpec((1, s, n * MLSTM_DV), lambda i, h: (i, 0, off // (n * MLSTM_DV) + h))
    return pl.pallas_call(
        _mlstm_kernel,
        grid=grid,
        in_specs=[qk_blk(OFF_Q), qk_blk(OFF_K), v_blk(OFF_V), v_blk(OFF_O),
                  pl.BlockSpec((1, s, GATE_W), lambda i, h: (i, 0, 0)),
                  pl.BlockSpec((1, GATE_W, s), lambda i, h: (i, 0, 0)),
                  pl.BlockSpec((n, 1, MLSTM_DV), lambda i, h: (h, 0, 0)),
                  side_in],
        out_specs=[pl.BlockSpec((1, s, n * MLSTM_DV), lambda i, h: (i, 0, h)), side_out],
        out_shape=[jax.ShapeDtypeStruct((b, s, MLSTM_HEADS * MLSTM_DV), BF16), side_shape],
        compiler_params=_cparams(("parallel", "parallel")),
        name="mlstm",
    )(proj, proj, proj, proj, gcol, grow, gain.reshape(MLSTM_HEADS, 1, MLSTM_DV), w_side)


def _conv_silu(ref, c, w_ref, b_ref):
    L = SSD_CHUNK
    cur = ref[0, pl.ds(c * L, L), :]
    if c == 0:
        halo = jnp.zeros((CONV_HALO, cur.shape[1]), F32)
    else:
        halo = ref[0, pl.ds(c * L - CONV_HALO, CONV_HALO), :]
    ext = jnp.concatenate([halo, cur], axis=0)
    assert SSD_CONV == 4
    w = w_ref[...]
    prev = pltpu.roll(ext, 1, axis=0)
    u = w[1:2, :] * ext + w[0:1, :] * prev
    acc = (b_ref[...] + w[3:4, :] * cur + w[2:3, :] * prev[CONV_HALO:, :]) + pltpu.roll(u, 2, axis=0)[CONV_HALO:, :]
    return _silu(acc)


def _per_head_lanes(vals, shape):
    lane = lax.broadcasted_iota(jnp.int32, shape, 1)
    out = jnp.broadcast_to(vals[SSD_HEADS_PER_GROUP - 1], shape)
    for r in range(SSD_HEADS_PER_GROUP - 2, -1, -1):
        out = jnp.where(lane < (r + 1) * SSD_HEADDIM, vals[r], out)
    return out


def _ssd_kernel(x_ref, b_ref, c_ref, z_ref, wx_ref, wb_ref, wc_ref, bx_ref, bb_ref, bc_ref,
                gcol_ref, grow_ref, d_ref, gain_ref, out_ref):
    g = pl.program_id(1)
    s_len = x_ref.shape[1]
    L = SSD_CHUNK
    R = SSD_HEADS_PER_GROUP
    ri = lax.broadcasted_iota(jnp.int32, (L, L), 0)
    ci = lax.broadcasted_iota(jnp.int32, (L, L), 1)
    causal = ci <= ri
    lane_head = lax.broadcasted_iota(jnp.int32, (L, SSD_GROUP_W), 1) // SSD_HEADDIM

    state = jnp.zeros((SSD_STATE, SSD_GROUP_W), F32)
    cs_prev_end = None

    for c in range(s_len // L):
        rows = pl.ds(c * L, L)
        xs = _conv_silu(x_ref, c, wx_ref, bx_ref)
        bm = _conv_silu(b_ref, c, wb_ref, bb_ref)
        cm = _conv_silu(c_ref, c, wc_ref, bc_ref)
        bmb = bm.astype(BF16)
        cmb = cm.astype(BF16)
        gc = gcol_ref[0, rows, :]
        dt_cols = [_lane_column(gc, GATE_DT + R * g + r) for r in range(R)]
        cs_cols = [_lane_column(gc, GATE_CS + R * g + r) for r in range(R)]
        base, half = (c * L) // CHUNK * CHUNK, (c * L) % CHUNK
        cs_rows = [grow_ref[0, pl.ds(GATE_CS + R * g + r, 1), pl.ds(base, CHUNK)][:, half:half + L]
                   for r in range(R)]
        cs_raw_end = [col[L - 1:L, :] for col in cs_cols]
        if half:
            cs_cols = [col - off for col, off in zip(cs_cols, cs_prev_end)]
            cs_rows = [row - off for row, off in zip(cs_rows, cs_prev_end)]
        cs_prev_end = cs_raw_end

        dt_full = _per_head_lanes(dt_cols, (L, SSD_GROUP_W))
        cs_full = _per_head_lanes(cs_cols, (L, SSD_GROUP_W))
        cs_end = _per_head_lanes([col[L - 1:L, :] for col in cs_cols], (1, SSD_GROUP_W))

        xd = xs * dt_full
        xdb = xd.astype(BF16)
        cb = _dot_nt(cmb, bmb)
        y = _dot(cmb, state.astype(BF16)) * jnp.exp2(cs_full)
        for r in range(R):
            lmat = jnp.where(causal, jnp.exp2(cs_cols[r] - cs_rows[r]), 0.0)
            xr = jnp.where(lane_head == r, xdb, jnp.zeros_like(xdb))
            y = y + _dot((cb * lmat).astype(BF16), xr)
        y = y + d_ref[...] * xs
        y = y * _silu(z_ref[0, rows, :])
        out_ref[0, rows, :] = _rms(y, gain_ref[...]).astype(out_ref.dtype)

        decay_end = jnp.exp2(cs_end - cs_full)
        state = jnp.exp2(cs_end) * state + _dot(bm.T.astype(BF16), (xd * decay_end).astype(BF16))


def ssd(proj, conv_w, conv_b, gcol, grow, d_row, gain_row):
    b, s, _ = proj.shape
    G = SSD_GROUPS
    wide = lambda off: pl.BlockSpec((1, s, SSD_GROUP_W), lambda i, g: (i, 0, off // SSD_GROUP_W + g))
    narrow = lambda off: pl.BlockSpec((1, s, SSD_STATE), lambda i, g: (i, 0, off // SSD_STATE + g))
    cw = lambda rows, width, off: pl.BlockSpec((rows, width), lambda i, g: (0, off // width + g))
    return pl.pallas_call(
        _ssd_kernel,
        grid=(b, G),
        in_specs=[wide(OFF_X), narrow(OFF_B), narrow(OFF_C), wide(OFF_Z),
                  cw(SSD_CONV, SSD_GROUP_W, CONV_OFF_X), cw(SSD_CONV, SSD_STATE, CONV_OFF_B),
                  cw(SSD_CONV, SSD_STATE, CONV_OFF_C),
                  cw(1, SSD_GROUP_W, CONV_OFF_X), cw(1, SSD_STATE, CONV_OFF_B), cw(1, SSD_STATE, CONV_OFF_C),
                  pl.BlockSpec((1, s, GATE_W), lambda i, g: (i, 0, 0)),
                  pl.BlockSpec((1, GATE_W, s), lambda i, g: (i, 0, 0)),
                  cw(1, SSD_GROUP_W, 0), cw(1, SSD_GROUP_W, 0)],
        out_specs=pl.BlockSpec((1, s, SSD_GROUP_W), lambda i, g: (i, 0, g)),
        out_shape=jax.ShapeDtypeStruct((b, s, SSD_D_INNER), BF16),
        compiler_params=_cparams(("parallel", "parallel")),
        name="ssd",
    )(proj, proj, proj, proj, conv_w, conv_w, conv_w, conv_b, conv_b, conv_b, gcol, grow, d_row, gain_row)


def _moba_kernel(q_ref, k_ref, v_ref, cos_ref, sin_ref, side_ref, out_ref, side_out_ref,
                 qs_ref, qb_ref, kb_ref, vt_ref, s_ref, pb_ref):
    side_out_ref[...] = side_ref[...].astype(BF16)
    for hh in range(MOBA_HEADS_PER_STEP):
        lanes = pl.ds(hh * ATTN_HEAD_DIM, ATTN_HEAD_DIM)
        _moba_head(q_ref.at[0, :, lanes], k_ref.at[0, :, lanes], v_ref.at[0, :, lanes], cos_ref, sin_ref,
                   out_ref.at[0, :, lanes], qs_ref, qb_ref, kb_ref, vt_ref, s_ref, pb_ref)


def _moba_head(q_ref, k_ref, v_ref, cos_ref, sin_ref, out_ref, qs_ref, qb_ref, kb_ref, vt_ref, s_ref, pb_ref):
    s_len = q_ref.shape[0]
    BLK = MOBA_BLOCK
    nb = s_len // BLK
    nbp = -(-nb // 8) * 8
    half = ATTN_HEAD_DIM // 2
    scale = ATTN_HEAD_DIM ** -0.5
    cos = cos_ref[...]
    sin = sin_ref[...]

    def rope(x):
        return x * cos + pltpu.roll(x, half, axis=1) * sin

    q = rope(q_ref[...])
    k = rope(k_ref[...])
    qs_ref[...] = q
    qb_ref[...] = (q * (scale * LOG2E)).astype(BF16)
    kb_ref[...] = k.astype(BF16)
    vt_ref[...] = v_ref[...].T.astype(BF16)

    row8 = lax.broadcasted_iota(jnp.int32, (nbp, ATTN_HEAD_DIM), 0)
    k_mean = jnp.zeros((nbp, ATTN_HEAD_DIM), F32)
    for j in range(nb):
        k_mean = jnp.where(row8 == j, jnp.mean(k[j * BLK:(j + 1) * BLK, :], axis=0, keepdims=True), k_mean)

    SUB = SUBLANES
    GRP = BLK // SUB
    gi = lax.broadcasted_iota(jnp.int32, (GRP, SUB, BLK), 0)
    si = lax.broadcasted_iota(jnp.int32, (GRP, SUB, BLK), 1)
    ci = lax.broadcasted_iota(jnp.int32, (GRP, SUB, BLK), 2)
    causal_bias = jnp.where(gi * SUB + si <= ci, 0.0, -jnp.inf)
    blk_row = lax.broadcasted_iota(jnp.int32, (nbp, BLK), 0)

    def scores(i):
        qrows = pl.ds(i * BLK, BLK)
        keep = None
        if i > MOBA_TOPK:
            gate = lax.dot_general(k_mean, qs_ref[qrows, :], (((1,), (1,)), ((), ())),
                                   precision=lax.Precision.HIGHEST, preferred_element_type=F32)
            rank = jnp.zeros((nbp, BLK), F32)
            for m in range(i):
                gm = gate[m:m + 1, :]
                ahead = (gm > gate) | ((gm == gate) & (blk_row > m))
                rank = rank + jnp.where(ahead, 1.0, 0.0)
            keep = rank < float(MOBA_TOPK)
        qb = qb_ref[qrows, :]
        m8 = None
        for j in range(i + 1):
            krows = pl.ds(j * BLK, BLK)
            sj = _dot_nt(kb_ref[krows, :], qb).reshape(GRP, SUB, BLK)
            if j == i:
                sj = sj + causal_bias
            elif keep is not None:
                sj = sj + jnp.broadcast_to(jnp.where(keep[j:j + 1, :], 0.0, -jnp.inf), (SUB, BLK))[None]
            s_ref[i % 2, krows, :] = sj.reshape(BLK, BLK)
            mj = jnp.max(sj, axis=0)
            m8 = mj if m8 is None else jnp.maximum(m8, mj)
        return m8

    def attend(i, m8):
        qrows = pl.ds(i * BLK, BLK)
        nk = (i + 1) * BLK
        m_row = jnp.broadcast_to(jnp.max(m8, axis=0, keepdims=True), (SUB, BLK))
        den8 = jnp.zeros((SUB, BLK), F32)
        for j in range(i + 1):
            krows = pl.ds(j * BLK, BLK)
            p = jnp.exp2(s_ref[i % 2, krows, :].reshape(GRP, SUB, BLK) - m_row[None])
            den8 = den8 + jnp.sum(p, axis=0)
            pb_ref[i % 2, krows, :] = p.reshape(BLK, BLK).astype(BF16)
        den = jnp.sum(den8, axis=0, keepdims=True)
        ot = _dot(vt_ref[:, pl.ds(0, nk)], pb_ref[i % 2, pl.ds(0, nk), :])
        out_ref[qrows, :] = (ot / den).T.astype(out_ref.dtype)

    m8_next = scores(0)
    for i in range(nb):
        m8_cur = m8_next
        if i + 1 < nb:
            m8_next = scores(i + 1)
        attend(i, m8_cur)


def moba(qkv, cos_full, sin_signed, w_side):
    b, s, _ = qkv.shape
    hd = ATTN_HEAD_DIM
    steps = ATTN_HEADS // MOBA_HEADS_PER_STEP
    grid = (b, steps)
    side_in, side_out, side_shape = _side_cast_specs(w_side, grid)
    blk = lambda part: pl.BlockSpec((1, s, MOBA_HEADS_PER_STEP * hd), lambda i, h: (i, 0, part * steps + h))
    return pl.pallas_call(
        _moba_kernel,
        grid=grid,
        in_specs=[blk(0), blk(1), blk(2),
                  pl.BlockSpec((s, hd), lambda i, h: (0, 0)),
                  pl.BlockSpec((s, hd), lambda i, h: (0, 0)),
                  side_in],
        out_specs=[pl.BlockSpec((1, s, MOBA_HEADS_PER_STEP * hd), lambda i, h: (i, 0, h)), side_out],
        out_shape=[jax.ShapeDtypeStruct((b, s, ATTN_HEADS * hd), BF16), side_shape],
        scratch_shapes=[pltpu.VMEM((s, hd), F32), pltpu.VMEM((s, hd), BF16), pltpu.VMEM((s, hd), BF16),
                        pltpu.VMEM((hd, s), BF16), pltpu.VMEM((2, s, MOBA_BLOCK), F32),
                        pltpu.VMEM((2, s, MOBA_BLOCK), BF16)],
        compiler_params=_cparams(("parallel", "parallel")),
        name="moba",
    )(qkv, qkv, qkv, cos_full, sin_signed, w_side)


def _rope_tables(s):
    half = ATTN_HEAD_DIM // 2
    inv = ROPE_THETA ** (-jnp.arange(half, dtype=F32) / half)
    ang = jnp.arange(s).astype(F32)[:, None] * inv[None, :]
    cos, sin = jnp.cos(ang), jnp.sin(ang)
    return jnp.concatenate([cos, cos], axis=-1), jnp.concatenate([-sin, sin], axis=-1)


def kernel(x, l0_norm_mix, l0_w_in, l0_mlstm_gate_bias, l0_mlstm_norm, l0_ssd_conv_w, l0_ssd_conv_b,
           l0_ssd_dt_bias, l0_ssd_a_log, l0_ssd_d, l0_ssd_norm, l0_w_out, l0_norm_ffn, l0_ffn_gate,
           l0_ffn_up, l0_ffn_down, l1_norm_mix, l1_w_qkv, l1_w_o, l1_norm_ffn, l1_ffn_gate, l1_ffn_up,
           l1_ffn_down, final_norm):
    b, s, d = x.shape
    m = b * s
    assert d == D_MODEL and s % CHUNK == 0 and s % MOBA_BLOCK == 0 and m % PROJ_TM == 0
    xr = x.reshape(m, d)

    w_in_t = l0_w_in.T
    w_gate_t = jnp.concatenate([w_in_t[IN_QKVO_END:IN_ZX_START], w_in_t[IN_ZX_END:IN_END],
                                jnp.zeros((GATE_W - GATE_COLS, d), F32)], axis=0)
    pad = jnp.zeros((GATE_W - GATE_COLS,), F32)
    bias_row = jnp.concatenate([l0_mlstm_gate_bias, l0_ssd_dt_bias, pad]).reshape(1, GATE_W)
    alog_row = jnp.concatenate([jnp.zeros((GATE_DT,), F32), l0_ssd_a_log, pad]).reshape(1, GATE_W)
    xn, gcol, grow = norm_gate_prep(x, l0_norm_mix, w_gate_t, bias_row, alog_row)
    xn = xn.reshape(m, d)
    proj_a = wres_matmul(xn, w_in_t, 0, IN_QKVO_END, w_transposed=True, tm=PROJ_TM, tn=PROJ_TN, out_dtype=F32,
                         name="in_proj_qkvo").reshape(b, s, -1)
    proj_b = wres_matmul(xn, w_in_t, IN_ZX_START, IN_ZX_END - IN_ZX_START, w_transposed=True, tm=PROJ_TM,
                         tn=PROJ_TN, out_dtype=F32, name="in_proj_zx").reshape(b, s, -1)

    hm, w_out_bf16 = mlstm(proj_a, gcol, grow, l0_mlstm_norm, l0_w_out)
    hm = hm.reshape(m, D_MODEL)
    d_row = jnp.repeat(l0_ssd_d, SSD_HEADDIM).reshape(1, SSD_D_INNER)
    ys = ssd(proj_b, l0_ssd_conv_w, l0_ssd_conv_b.reshape(1, -1), gcol, grow, d_row,
             l0_ssd_norm.reshape(1, SSD_D_INNER)).reshape(m, SSD_D_INNER)
    xr, xn = matmul_residual_norm([hm, ys], w_out_bf16, xr, l0_norm_ffn, tm=OUT_PROJ_TM, emit_x=True,
                                  norm_dtype=BF16, name="l0_out_proj")
    hff, w_down_bf16 = swiglu_up(xn, l0_ffn_gate, l0_ffn_up, l0_ffn_down, tm=FFN_UP_TM, tn=FFN_UP_TN,
                                 name="l0_ffn_up")
    xr, xn = matmul_residual_norm([hff], w_down_bf16, xr, l1_norm_mix, tm=FFN_DOWN_TM, emit_x=True,
                                  norm_dtype=BF16, name="l0_ffn_down")

    qkv = wres_matmul(xn, l1_w_qkv, 0, 3 * D_MODEL, w_transposed=False, tm=PROJ_TM, tn=PROJ_TN, out_dtype=F32,
                      name="qkv_proj").reshape(b, s, 3 * D_MODEL)
    cos_full, sin_signed = _rope_tables(s)
    att, w_o_bf16 = moba(qkv, cos_full, sin_signed, l1_w_o)
    xr, xn = matmul_residual_norm([att.reshape(m, D_MODEL)], w_o_bf16, xr, l1_norm_ffn, tm=OUT_PROJ_TM,
                                  emit_x=True, norm_dtype=BF16, name="l1_out_proj")
    hff, w_down_bf16 = swiglu_up(xn, l1_ffn_gate, l1_ffn_up, l1_ffn_down, tm=FFN_UP_TM, tn=FFN_UP_TN,
                                 name="l1_ffn_up")
    (out,) = matmul_residual_norm([hff], w_down_bf16, xr, final_norm, tm=FFN_DOWN_TM, emit_x=False,
                                  norm_dtype=F32, name="l1_ffn_down")
    return out.reshape(b, s, d)
```
